```python
import jax, jax.numpy as jnp
from jax import lax
import numpy as np

D_MODEL = 1024
BATCH = 4
SEQ = 4096
DEPTH = 4
DEC_BATCH = 32
DEC_SEQ = 16
PAST_LEN = 1024

CHUNK = 64
Q_BLOCK = 2 * CHUNK
HEAD_DIM = 64
H_R = 8
H_F = 8
W_R = H_R * HEAD_DIM
W_F = H_F * HEAD_DIM
DECAY_LORA = 64
AAA_LORA = 64
GATE_LORA = 128
R_COLS = 3 * W_R + DECAY_LORA + AAA_LORA + GATE_LORA
F_COLS = 3 * W_F + H_F
G_COLS = 2 * D_MODEL
P_TOT = R_COLS + F_COLS + G_COLS
SPLIT_R = [W_R, 2 * W_R, 3 * W_R, 3 * W_R + DECAY_LORA, 3 * W_R + DECAY_LORA + AAA_LORA]
D_FF = -(-8 * D_MODEL // (3 * 256)) * 256
EPS = 1e-6
GN_EPS = 64e-5
SCALE = HEAD_DIM ** -0.5

kernel_name = "rwkv7_fox_gated_streaming_step"


def rms_norm(x, g):
    xf = x.astype(jnp.float32)
    y = xf * lax.rsqrt(jnp.mean(xf * xf, axis=-1, keepdims=True) + EPS)
    return (y * g.astype(jnp.float32)).astype(x.dtype)


def _delta_scan(S0, r, w, k, v, a, b):
    def step(S, inp):
        r_t, w_t, k_t, v_t, a_t, b_t = inp
        sa = jnp.einsum('bhij,bhj->bhi', S, a_t)
        S = S * w_t[:, :, None, :] + sa[..., None] * b_t[:, :, None, :] + v_t[..., None] * k_t[:, :, None, :]
        return S, jnp.einsum('bhij,bhj->bhi', S, r_t)
    xs = tuple(jnp.swapaxes(t, 0, 1) for t in (r, w, k, v, a, b))
    S, ys = lax.scan(step, S0, xs)
    return S, jnp.swapaxes(ys, 0, 1)


def _rwkv7(pr, prev_row, S0, W, l):
    Bn, T, _ = pr.shape
    shifted = jnp.concatenate([prev_row.astype(pr.dtype), pr[:, :-1]], axis=1)
    u = (pr + (shifted - pr) * W['rwkv_mu'][l]).astype(jnp.float32)
    r, k, v, wl, al, gl = jnp.split(u, SPLIT_R, axis=-1)
    w_raw = W['rwkv_w0'][l] + jnp.tanh(wl) @ W['rwkv_w2'][l]
    decay = jnp.exp(-jnp.exp(-jax.nn.softplus(-w_raw) - 0.5))
    a = jax.nn.sigmoid(W['rwkv_a0'][l] + al @ W['rwkv_a2'][l])
    g = jax.nn.sigmoid(gl) @ W['rwkv_g2'][l]
    heads = lambda t: t.reshape(Bn, T, H_R, HEAD_DIM)
    kk = heads(k * W['rwkv_k_k'][l])
    kk = kk * lax.rsqrt(jnp.sum(kk * kk, axis=-1, keepdims=True) + 1e-12)
    k = heads(k * (1.0 + (a - 1.0) * W['rwkv_k_a'][l]))
    r, v, a, decay = heads(r), heads(v), heads(a), heads(decay)
    S, y = _delta_scan(S0.astype(jnp.float32), r, decay, k, v, -kk, kk * a)
    mu_y = jnp.mean(y, axis=-1, keepdims=True)
    var = jnp.mean(jnp.square(y - mu_y), axis=-1, keepdims=True)
    y = ((y - mu_y) * lax.rsqrt(var + GN_EPS)).reshape(Bn, T, W_R) * W['rwkv_lnx_g'][l] + W['rwkv_lnx_b'][l]
    bonus = jnp.sum(r * k * W['rwkv_r_k'][l], axis=-1, keepdims=True) * v
    out = (y + bonus.reshape(Bn, T, W_R)) * g
    return out.astype(pr.dtype), S


def _fox_attend(q, k, v, cq, ck, q_pos):
    s = jnp.einsum('bqhd,bkhd->bhqk', q, k, preferred_element_type=jnp.float32) * SCALE
    s = s + cq[..., :, None] - ck[..., None, :]
    mask = jnp.arange(k.shape[1])[None, :] <= q_pos[:, None]
    p = jax.nn.softmax(jnp.where(mask, s, -jnp.inf), axis=-1)
    return jnp.einsum('bhqk,bkhd->bqhd', p.astype(v.dtype), v)


def _fox_prompt(q, k, v, logf):
    Bn, S = q.shape[:2]
    nb = S // Q_BLOCK
    cT = jnp.swapaxes(lax.cumsum(logf, axis=1), 1, 2)
    qb = jnp.swapaxes(q.reshape(Bn, nb, Q_BLOCK, H_F, HEAD_DIM), 0, 1)
    cqb = jnp.transpose(cT.reshape(Bn, H_F, nb, Q_BLOCK), (2, 0, 1, 3))
    def block(args):
        q_blk, cq_blk, i = args
        return _fox_attend(q_blk, k, v, cq_blk, cT, i * Q_BLOCK + jnp.arange(Q_BLOCK))
    o = lax.map(block, (qb, cqb, jnp.arange(nb)))
    return jnp.swapaxes(o, 0, 1).reshape(Bn, S, W_F)


def _fox_sample(q, k, v, logf, k_cache, v_cache, logf_cache):
    Bn, T = q.shape[:2]
    P = k_cache.shape[1]
    k_all = jnp.concatenate([k_cache.astype(k.dtype), k], axis=1)
    v_all = jnp.concatenate([v_cache.astype(v.dtype), v], axis=1)
    c = lax.cumsum(jnp.concatenate([logf_cache.astype(jnp.float32), logf], axis=1), axis=1)
    cT = jnp.swapaxes(c, 1, 2)
    o = _fox_attend(q, k_all, v_all, cT[..., P:], cT, P + jnp.arange(T))
    return o.reshape(Bn, T, W_F)


def _trunk(x, W, past):
    Bn, T, _ = x.shape
    ks, vs, lfs, Ss, shs = [], [], [], [], []
    for l in range(DEPTH):
        h = rms_norm(x, W['norm1_g'][l])
        p = h @ W['w_in'][l]
        pr = p[..., :R_COLS]
        pf = p[..., R_COLS:R_COLS + F_COLS]
        pg = p[..., R_COLS + F_COLS:]
        if past is None:
            prev_row = jnp.zeros((Bn, 1, R_COLS), x.dtype)
            S0 = jnp.zeros((Bn, H_R, HEAD_DIM, HEAD_DIM), jnp.float32)
        else:
            S0, prev_row = past[3][l], past[4][l]
        o_a, S_new = _rwkv7(pr, prev_row, S0, W, l)
        q, k, v, fl = jnp.split(pf, [W_F, 2 * W_F, 3 * W_F], axis=-1)
        q, k, v = (t.reshape(Bn, T, H_F, HEAD_DIM) for t in (q, k, v))
        logf = jax.nn.log_sigmoid((fl + W['fox_bf'][l]).astype(jnp.float32))
        if past is None:
            o_b = _fox_prompt(q, k, v, logf)
        else:
            o_b = _fox_sample(q, k, v, logf, past[0][l], past[1][l], past[2][l])
        g_a, g_b = jnp.split(jax.nn.sigmoid(pg), 2, axis=-1)
        m = g_a * (o_a @ W['p_a'][l]) + g_b * (o_b @ W['p_b'][l])
        x = x + m @ W['w_out'][l]
        h2 = rms_norm(x, W['norm2_g'][l])
        x = x + (jax.nn.silu(h2 @ W['w_gate'][l]) * (h2 @ W['w_up'][l])) @ W['w_down'][l]
        ks.append(k)
        vs.append(v)
        lfs.append(logf.astype(x.dtype))
        Ss.append(S_new.astype(x.dtype))
        shs.append(pr[:, -1:])
    y = rms_norm(x, W['final_g'])
    return y, jnp.stack(ks), jnp.stack(vs), jnp.stack(lfs), jnp.stack(Ss), jnp.stack(shs)


def setup_inputs(seed: int = 0) -> dict:
    key = jax.random.key(seed)
    ks = iter(jax.random.split(key, 40))
    nrm = lambda shape, scale: jax.random.normal(next(ks), shape, jnp.float32) * scale
    uni = lambda shape, lo, hi: jax.random.uniform(next(ks), shape, jnp.float32, lo, hi)
    return {
        'x_prompt': nrm((BATCH, SEQ, D_MODEL), 1.0),
        'x_sample': nrm((DEC_BATCH, DEC_SEQ, D_MODEL), 1.0),
        'cache_fox_k': nrm((DEPTH, DEC_BATCH, PAST_LEN, H_F, HEAD_DIM), 1.0),
        'cache_fox_v': nrm((DEPTH, DEC_BATCH, PAST_LEN, H_F, HEAD_DIM), 1.0),
        'cache_fox_logf': jax.nn.log_sigmoid(3.0 + nrm((DEPTH, DEC_BATCH, PAST_LEN, H_F), 1.0)),
        'state_rwkv': nrm((DEPTH, DEC_BATCH, H_R, HEAD_DIM, HEAD_DIM), 0.5),
        'state_shift': nrm((DEPTH, DEC_BATCH, 1, R_COLS), 1.0),
        'norm1_g': 1.0 + nrm((DEPTH, D_MODEL), 0.1),
        'w_in': nrm((DEPTH, D_MODEL, P_TOT), D_MODEL ** -0.5),
        'rwkv_mu': uni((DEPTH, R_COLS), 0.0, 1.0),
        'rwkv_w0': uni((DEPTH, W_R), -6.5, -1.5),
        'rwkv_w2': nrm((DEPTH, DECAY_LORA, W_R), 0.1),
        'rwkv_a0': nrm((DEPTH, W_R), 0.1),
        'rwkv_a2': nrm((DEPTH, AAA_LORA, W_R), 0.5 * AAA_LORA ** -0.5),
        'rwkv_g2': nrm((DEPTH, GATE_LORA, W_R), GATE_LORA ** -0.5),
        'rwkv_k_k': 0.85 + nrm((DEPTH, W_R), 0.05),
        'rwkv_k_a': 1.0 + nrm((DEPTH, W_R), 0.05),
        'rwkv_r_k': nrm((DEPTH, H_R, HEAD_DIM), 0.1),
        'rwkv_lnx_g': 1.0 + nrm((DEPTH, W_R), 0.1),
        'rwkv_lnx_b': nrm((DEPTH, W_R), 0.01),
        'fox_bf': 3.0 + nrm((DEPTH, H_F), 0.5),
        'p_a': nrm((DEPTH, W_R, D_MODEL), W_R ** -0.5),
        'p_b': nrm((DEPTH, W_F, D_MODEL), W_F ** -0.5),
        'w_out': nrm((DEPTH, D_MODEL, D_MODEL), D_MODEL ** -0.5),
        'norm2_g': 1.0 + nrm((DEPTH, D_MODEL), 0.1),
        'w_gate': nrm((DEPTH, D_MODEL, D_FF), D_MODEL ** -0.5),
        'w_up': nrm((DEPTH, D_MODEL, D_FF), D_MODEL ** -0.5),
        'w_down': nrm((DEPTH, D_FF, D_MODEL), D_FF ** -0.5),
        'final_g': 1.0 + nrm((D_MODEL,), 0.1),
    }


def reference(x_prompt, x_sample, cache_fox_k, cache_fox_v, cache_fox_logf, state_rwkv, state_shift,
              norm1_g, w_in, rwkv_mu, rwkv_w0, rwkv_w2, rwkv_a0, rwkv_a2, rwkv_g2, rwkv_k_k, rwkv_k_a,
              rwkv_r_k, rwkv_lnx_g, rwkv_lnx_b, fox_bf, p_a, p_b, w_out, norm2_g, w_gate, w_up, w_down,
              final_g):
    W = dict(norm1_g=norm1_g, w_in=w_in, rwkv_mu=rwkv_mu, rwkv_w0=rwkv_w0, rwkv_w2=rwkv_w2,
             rwkv_a0=rwkv_a0, rwkv_a2=rwkv_a2, rwkv_g2=rwkv_g2, rwkv_k_k=rwkv_k_k, rwkv_k_a=rwkv_k_a,
             rwkv_r_k=rwkv_r_k, rwkv_lnx_g=rwkv_lnx_g, rwkv_lnx_b=rwkv_lnx_b, fox_bf=fox_bf,
             p_a=p_a, p_b=p_b, w_out=w_out, norm2_g=norm2_g, w_gate=w_gate, w_up=w_up,
             w_down=w_down, final_g=final_g)
    y_prompt, kp, vp, lfp, sp, shp = _trunk(x_prompt, W, None)
    y_sample, kd, vd, lfd, sd, shd = _trunk(
        x_sample, W, (cache_fox_k, cache_fox_v, cache_fox_logf, state_rwkv, state_shift))
    return (y_prompt, y_sample, kp, vp, lfp, sp, shp, kd, vd, lfd, sd, shd)
```

```python
import functools

import jax
import jax.numpy as jnp
from jax import lax
from jax.experimental import pallas as pl
from jax.experimental.pallas import tpu as pltpu

F32 = jnp.float32
BF16 = jnp.bfloat16

HEAD_DIM = 64
N_HEADS = 8
W_H = N_HEADS * HEAD_DIM
DECAY_LORA = 64
AAA_LORA = 64
GATE_LORA = 128
R_COLS = 3 * W_H + DECAY_LORA + AAA_LORA + GATE_LORA
EPS = 1e-6
GN_EPS = 64e-5
SCALE = HEAD_DIM ** -0.5

LANES = 128
VMEM_LIMIT_BYTES = 56 * 1024 * 1024

COL_GA = 0
COL_GB = 1024
COL_Q = 2048
COL_K = 2560
COL_V = 3072
COL_R = 3584
COL_FL = 5376
P_COLS = 5632
PROMPT_CHUNK = 64
ATT_BLOCK = 256
CUM_BLOCK = 512


def _cparams(sem):
    return pltpu.CompilerParams(dimension_semantics=sem, vmem_limit_bytes=VMEM_LIMIT_BYTES)


def _mm(a, b):
    return jnp.dot(a.astype(BF16), b.astype(BF16), preferred_element_type=F32)


def _mm_nt(a, b):
    return lax.dot_general(a.astype(BF16), b.astype(BF16), (((1,), (1,)), ((), ())),
                           preferred_element_type=F32)


def _mm_tn(a, b):
    return lax.dot_general(a.astype(BF16), b.astype(BF16), (((0,), (0,)), ((), ())),
                           preferred_element_type=F32)


def _split3(x):
    hi = x.astype(BF16)
    r = x - hi.astype(F32)
    mid = r.astype(BF16)
    lo = (r - mid.astype(F32)).astype(BF16)
    return hi, mid, lo


def _mm_exact_rhs(m01, x):
    hi, mid, lo = _split3(x)
    d = lambda p: jnp.dot(m01, p, preferred_element_type=F32)
    return d(hi) + d(mid) + d(lo)


def _log_sigmoid(z):
    return jnp.minimum(z, 0.0) - jnp.log1p(jnp.exp(-jnp.abs(z)))


def _sigmoid(z):
    return 1.0 / (1.0 + jnp.exp(-z))


def _tile(m, cands=(512, 256, 128, 64, 32, 16, 8)):
    for t in cands:
        if m % t == 0:
            return t
    raise ValueError(m)


def _norm_matmul_kernel(x_ref, g_ref, w_ref, o_ref, h_ref):
    @pl.when(pl.program_id(1) == 0)
    def _():
        x = x_ref[...]
        y = x * lax.rsqrt(jnp.mean(x * x, axis=-1, keepdims=True) + EPS) * g_ref[...]
        h_ref[...] = y.astype(BF16)

    o_ref[...] = jnp.dot(h_ref[...], w_ref[...], preferred_element_type=F32)


def _norm_matmul(x, g, w, tm, tn):
    m, d = x.shape
    n = w.shape[1]
    return pl.pallas_call(
        _norm_matmul_kernel,
        grid=(m // tm, n // tn),
        in_specs=[pl.BlockSpec((tm, d), lambda i, j: (i, 0)),
                  pl.BlockSpec((1, d), lambda i, j: (0, 0)),
                  pl.BlockSpec((d, tn), lambda i, j: (0, j))],
        out_specs=pl.BlockSpec((tm, tn), lambda i, j: (i, j)),
        out_shape=jax.ShapeDtypeStruct((m, n), F32),
        scratch_shapes=[pltpu.VMEM((tm, d), BF16)],
        compiler_params=_cparams(("parallel", "arbitrary")),
        name="in_proj",
    )(x, g, w)


def _rwkv_kernel(pr_ref, shift_ref, s0_ref, mu_ref, w0_ref, w2_ref, a0_ref, a2_ref, g2_ref,
                 kk_ref, ka_ref, rk_ref, lg_ref, lb_ref,
                 o_ref, s_out_ref, shift_out_ref, state_ref, prev_ref, *, chunk, n_chunks):
    c = pl.program_id(1)
    C = chunk

    @pl.when(c == 0)
    def _():
        state_ref[...] = s0_ref[0]
        prev_ref[...] = shift_ref[0]

    pr = pr_ref[...]
    row = lax.broadcasted_iota(jnp.int32, (C, 1), 0)
    shifted = jnp.where(row == 0, prev_ref[...], pltpu.roll(pr, 1, 0))
    u = pr + (shifted - pr) * mu_ref[...]
    prev_ref[...] = pr[C - 1:C, :]

    r = u[:, 0:W_H]
    k = u[:, W_H:2 * W_H]
    v = u[:, 2 * W_H:3 * W_H]
    wl = u[:, 3 * W_H:3 * W_H + DECAY_LORA]
    al = u[:, 3 * W_H + DECAY_LORA:3 * W_H + DECAY_LORA + AAA_LORA]
    gl = u[:, 3 * W_H + DECAY_LORA + AAA_LORA:R_COLS]

    w_raw = w0_ref[...] + _mm(jnp.tanh(wl), w2_ref[...])
    logw = -jnp.exp(_log_sigmoid(w_raw) - 0.5)
    a_sig = _sigmoid(a0_ref[...] + _mm(al, a2_ref[...]))
    gate = _mm(_sigmoid(gl), g2_ref[...])
    kk_all = k * kk_ref[...]
    k_new = k * (1.0 + (a_sig - 1.0) * ka_ref[...])

    ti = lax.broadcasted_iota(jnp.int32, (C, C), 0)
    si = lax.broadcasted_iota(jnp.int32, (C, C), 1)
    lower_incl = si <= ti
    lower_strict = si < ti
    g_cum = _mm_exact_rhs(lower_incl.astype(BF16), logw)
    g_excl = g_cum - logw
    e_incl = jnp.exp(g_cum)
    e_excl = jnp.exp(g_excl)
    e_inv = jnp.exp(-g_cum)
    p_end = e_incl[C - 1:C, :]

    n_levels = C.bit_length() - 1
    for h in range(N_HEADS):
        sl = slice(h * HEAD_DIM, (h + 1) * HEAD_DIM)
        r_h, k_h, v_h = r[:, sl], k_new[:, sl], v[:, sl]
        kk_h = kk_all[:, sl]
        kk_h = kk_h * lax.rsqrt(jnp.sum(kk_h * kk_h, axis=-1, keepdims=True) + 1e-12)
        a_h = -kk_h
        b_h = kk_h * a_sig[:, sl]
        a_t = a_h * e_excl[:, sl]
        r_t = r_h * e_incl[:, sl]
        b_hat = b_h * e_inv[:, sl]
        k_hat = k_h * e_inv[:, sl]
        pc = p_end[:, sl]
        b_til = b_hat * pc
        k_til = k_hat * pc

        left = jnp.concatenate([a_t, r_t], axis=0)
        ab = _mm_nt(left, b_hat)
        ak = _mm_nt(left, k_hat)
        n_mat = jnp.where(lower_strict, ab[:C], 0.0)
        a_ak = jnp.where(lower_strict, ak[:C], 0.0)
        a_rb = jnp.where(lower_incl, ab[C:], 0.0)
        a_rk = jnp.where(lower_incl, ak[C:], 0.0)

        x_a = a_t
        x_v = _mm(a_ak, v_h)
        n_pow = n_mat
        for lvl in range(n_levels):
            x_a = x_a + _mm(n_pow, x_a)
            x_v = x_v + _mm(n_pow, x_v)
            if lvl + 1 < n_levels:
                n_pow = _mm(n_pow, n_pow)

        s_prev = state_ref[h]
        u_h = _mm_nt(x_a, s_prev) + x_v
        y_h = _mm_nt(r_t, s_prev) + _mm(a_rb, u_h) + _mm(a_rk, v_h)
        state_ref[h] = s_prev * pc + _mm_tn(u_h, b_til) + _mm_tn(v_h, k_til)

        mu_y = jnp.mean(y_h, axis=-1, keepdims=True)
        yc = y_h - mu_y
        var = jnp.mean(yc * yc, axis=-1, keepdims=True)
        yn = yc * lax.rsqrt(var + GN_EPS) * lg_ref[:, sl] + lb_ref[:, sl]
        bonus = jnp.sum(r_h * k_h * rk_ref[:, sl], axis=-1, keepdims=True) * v_h
        o_ref[:, sl] = ((yn + bonus) * gate[:, sl]).astype(o_ref.dtype)

    @pl.when(c == n_chunks - 1)
    def _():
        s_out_ref[0] = state_ref[...]
        shift_out_ref[0] = pr[C - 1:C, :]


def _rwkv(p_all, o_prev, shift_in, s0, lw, *, row0, n_seq, seq_len, chunk):
    n_chunks = seq_len // chunk
    rb0 = row0 // chunk
    vec = lambda n: pl.BlockSpec((1, n), lambda b, c: (0, 0))
    mat = lambda k, n: pl.BlockSpec((k, n), lambda b, c: (0, 0))
    m = p_all.shape[0]
    kern = functools.partial(_rwkv_kernel, chunk=chunk, n_chunks=n_chunks)
    in_specs = [
        pl.BlockSpec((chunk, R_COLS), lambda b, c: (rb0 + b * n_chunks + c, COL_R // R_COLS)),
        pl.BlockSpec((1, 1, R_COLS), lambda b, c: (b, 0, 0)),
        pl.BlockSpec((1, N_HEADS, HEAD_DIM, HEAD_DIM), lambda b, c: (b, 0, 0, 0)),
        vec(R_COLS), vec(W_H), mat(DECAY_LORA, W_H), vec(W_H), mat(AAA_LORA, W_H),
        mat(GATE_LORA, W_H), vec(W_H), vec(W_H), vec(W_H), vec(W_H), vec(W_H),
    ]
    args = [p_all, shift_in, s0, lw['mu'], lw['w0'], lw['w2'], lw['a0'], lw['a2'], lw['g2'],
            lw['k_k'], lw['k_a'], lw['r_k'], lw['lnx_g'], lw['lnx_b']]
    aliases = {}
    if o_prev is not None:
        in_specs.append(pl.BlockSpec(memory_space=pl.ANY))
        args.append(o_prev)
        aliases = {len(args) - 1: 0}
        kern_fn = lambda *refs: kern(*refs[:14], *refs[15:])
    else:
        kern_fn = kern
    return pl.pallas_call(
        kern_fn,
        grid=(n_seq, n_chunks),
        in_specs=in_specs,
        out_specs=[pl.BlockSpec((chunk, W_H), lambda b, c: (rb0 + b * n_chunks + c, 0)),
                   pl.BlockSpec((1, N_HEADS, HEAD_DIM, HEAD_DIM), lambda b, c: (b, 0, 0, 0)),
                   pl.BlockSpec((1, 1, R_COLS), lambda b, c: (b, 0, 0))],
        out_shape=[jax.ShapeDtypeStruct((m, W_H), BF16),
                   jax.ShapeDtypeStruct((n_seq, N_HEADS, HEAD_DIM, HEAD_DIM), F32),
                   jax.ShapeDtypeStruct((n_seq, 1, R_COLS), F32)],
        scratch_shapes=[pltpu.VMEM((N_HEADS, HEAD_DIM, HEAD_DIM), F32),
                        pltpu.VMEM((1, R_COLS), F32)],
        input_output_aliases=aliases,
        compiler_params=_cparams(("parallel", "arbitrary")),
        name=f"rwkv_c{chunk}",
    )(*args)


def _logf_cumsum_kernel(fl_ref, bf_ref, logf_ref, c_ref, ct_ref, carry_ref):
    @pl.when(pl.program_id(1) == 0)
    def _():
        carry_ref[...] = jnp.zeros_like(carry_ref)

    tc = fl_ref.shape[0]
    logf = _log_sigmoid(fl_ref[...] + bf_ref[...])
    logf_ref[...] = logf
    ti = lax.broadcasted_iota(jnp.int32, (tc, tc), 0)
    si = lax.broadcasted_iota(jnp.int32, (tc, tc), 1)
    c = _mm_exact_rhs((si <= ti).astype(BF16), logf) + carry_ref[...]
    c_ref[...] = c
    carry_ref[...] = c[tc - 1:tc, :]
    ct_ref[0] = c.T[0:N_HEADS, :]


def _logf_cumsum(p_all, bf_pad, *, n_seq, seq_len):
    tc = _tile(seq_len, (CUM_BLOCK, 256, 128))
    nb = seq_len // tc
    rows = n_seq * seq_len
    return pl.pallas_call(
        _logf_cumsum_kernel,
        grid=(n_seq, nb),
        in_specs=[pl.BlockSpec((tc, LANES), lambda b, j: (b * nb + j, COL_FL // LANES)),
                  pl.BlockSpec((1, LANES), lambda b, j: (0, 0))],
        out_specs=[pl.BlockSpec((tc, LANES), lambda b, j: (b * nb + j, 0)),
                   pl.BlockSpec((tc, LANES), lambda b, j: (b * nb + j, 0)),
                   pl.BlockSpec((1, N_HEADS, tc), lambda b, j: (b, 0, j))],
        out_shape=[jax.ShapeDtypeStruct((rows, LANES), F32),
                   jax.ShapeDtypeStruct((rows, LANES), F32),
                   jax.ShapeDtypeStruct((n_seq, N_HEADS, seq_len), F32)],
        scratch_shapes=[pltpu.VMEM((1, LANES), F32)],
        compiler_params=_cparams(("parallel", "arbitrary")),
        name="logf_cumsum",
    )(p_all, bf_pad)


def _fox_prompt_kernel(q_ref, k_ref, v_ref, c_ref, ct_ref, o_ref, *, blk):
    hp = pl.program_id(1)
    i = pl.program_id(2)
    lane = lax.broadcasted_iota(jnp.int32, (blk, LANES), 1)
    rowi = lax.broadcasted_iota(jnp.int32, (blk, blk), 0)
    coli = lax.broadcasted_iota(jnp.int32, (blk, blk), 1)
    c_blk = c_ref[...]
    for hh in range(2):
        head = 2 * hp + hh
        sl = slice(hh * HEAD_DIM, (hh + 1) * HEAD_DIM)
        q = (q_ref[:, sl] * SCALE).astype(BF16)
        cq = jnp.sum(jnp.where(lane == head, c_blk, 0.0), axis=-1, keepdims=True)

        def step(j, carry, masked):
            m, l, acc = carry
            start = pl.multiple_of(j * blk, blk)
            kb = k_ref[pl.ds(start, blk), sl]
            vb = v_ref[pl.ds(start, blk), sl]
            ck = ct_ref[0, pl.ds(head, 1), pl.ds(start, blk)]
            s = _mm_nt(q, kb) + (cq - ck)
            if masked:
                s = jnp.where(coli <= rowi, s, -jnp.inf)
            m_new = jnp.maximum(m, jnp.max(s, axis=-1, keepdims=True))
            alpha = jnp.exp(m - m_new)
            p = jnp.exp(s - m_new)
            l = alpha * l + jnp.sum(p, axis=-1, keepdims=True)
            acc = alpha * acc + _mm(p, vb)
            return m_new, l, acc

        init = (jnp.full((blk, 1), -jnp.inf, F32), jnp.zeros((blk, 1), F32),
                jnp.zeros((blk, HEAD_DIM), F32))
        carry = lax.fori_loop(0, i, lambda j, cr: step(j, cr, False), init)
        m, l, acc = step(i, carry, True)
        o_ref[:, sl] = (acc / l).astype(o_ref.dtype)


def _fox_prompt(p_all, c, ct, *, n_seq, seq_len):
    blk = _tile(seq_len, (ATT_BLOCK, 128))
    nq = seq_len // blk
    m = p_all.shape[0]
    kern = functools.partial(_fox_prompt_kernel, blk=blk)
    return pl.pallas_call(
        kern,
        grid=(n_seq, N_HEADS // 2, nq),
        in_specs=[pl.BlockSpec((blk, LANES), lambda b, hp, i: (b * nq + i, COL_Q // LANES + hp)),
                  pl.BlockSpec((seq_len, LANES), lambda b, hp, i: (b, COL_K // LANES + hp)),
                  pl.BlockSpec((seq_len, LANES), lambda b, hp, i: (b, COL_V // LANES + hp)),
                  pl.BlockSpec((blk, LANES), lambda b, hp, i: (b * nq + i, 0)),
                  pl.BlockSpec((1, N_HEADS, seq_len), lambda b, hp, i: (b, 0, 0))],
        out_specs=pl.BlockSpec((blk, LANES), lambda b, hp, i: (b * nq + i, hp)),
        out_shape=jax.ShapeDtypeStruct((m, W_H), BF16),
        compiler_params=_cparams(("parallel", "parallel", "arbitrary")),
        name="fox_prompt",
    )(p_all, p_all, p_all, c, ct)


def _fox_sample_kernel(q_ref, k_ref, v_ref, fl_ref, bf_ref, ck_ref, cv_ref, clt_ref, ob_prev_ref,
                       o_ref, logf_ref, *, t_new, past):
    del ob_prev_ref
    T = t_new
    logf = _log_sigmoid(fl_ref[...] + bf_ref[...])
    logf_ref[...] = logf
    logf_pad = jnp.concatenate([logf, jnp.zeros((LANES - T, LANES), F32)], axis=0)
    ti = lax.broadcasted_iota(jnp.int32, (LANES, LANES), 0)
    si = lax.broadcasted_iota(jnp.int32, (LANES, LANES), 1)
    cnew_pad = _mm_exact_rhs((si <= ti).astype(BF16), logf_pad)
    cnew_t = cnew_pad.T
    cnew = cnew_pad[0:T, :]
    suf = clt_ref[0, 0]
    pos = lax.broadcasted_iota(jnp.int32, suf.shape, 1)
    total = suf
    d = 1
    while d < past:
        total = total + jnp.where(pos + d < past, pltpu.roll(total, past - d, 1), 0.0)
        d *= 2
    suf = total - suf

    lane = lax.broadcasted_iota(jnp.int32, (T, LANES), 1)
    rowi = lax.broadcasted_iota(jnp.int32, (T, T), 0)
    coli = lax.broadcasted_iota(jnp.int32, (T, T), 1)
    for h in range(N_HEADS):
        sl = slice(h * HEAD_DIM, (h + 1) * HEAD_DIM)
        q = q_ref[:, sl] * SCALE
        cq = jnp.sum(jnp.where(lane == h, cnew, 0.0), axis=-1, keepdims=True)
        s_c = _mm_nt(q, ck_ref[0, 0, :, sl]) + (cq + suf[h:h + 1, :])
        s_n = _mm_nt(q, k_ref[:, sl]) + (cq - cnew_t[h:h + 1, 0:T])
        s_n = jnp.where(coli <= rowi, s_n, -jnp.inf)
        m = jnp.maximum(jnp.max(s_c, axis=-1, keepdims=True), jnp.max(s_n, axis=-1, keepdims=True))
        p_c = jnp.exp(s_c - m)
        p_n = jnp.exp(s_n - m)
        l = jnp.sum(p_c, axis=-1, keepdims=True) + jnp.sum(p_n, axis=-1, keepdims=True)
        o = _mm(p_c, cv_ref[0, 0, :, sl]) + _mm(p_n, v_ref[:, sl])
        o_ref[:, sl] = (o / l).astype(o_ref.dtype)


def _fox_sample(p_all, bf_pad, cache_k, cache_v, cache_logf_t, o_prev, layer, *, row0, n_seq, t_new):
    past = cache_k.shape[2]
    rb0 = row0 // t_new
    kern = functools.partial(_fox_sample_kernel, t_new=t_new, past=past)
    blk512 = lambda col: pl.BlockSpec((t_new, W_H), lambda b: (rb0 + b, col // W_H))
    return pl.pallas_call(
        kern,
        grid=(n_seq,),
        in_specs=[blk512(COL_Q), blk512(COL_K), blk512(COL_V),
                  pl.BlockSpec((t_new, LANES), lambda b: (rb0 + b, COL_FL // LANES)),
                  pl.BlockSpec((1, LANES), lambda b: (0, 0)),
                  pl.BlockSpec((1, 1, past, W_H), lambda b: (layer, b, 0, 0)),
                  pl.BlockSpec((1, 1, past, W_H), lambda b: (layer, b, 0, 0)),
                  pl.BlockSpec((1, 1, N_HEADS, past), lambda b: (layer, b, 0, 0)),
                  pl.BlockSpec(memory_space=pl.ANY)],
        out_specs=[pl.BlockSpec((t_new, LANES * 4), lambda b: (rb0 + b, 0)),
                   pl.BlockSpec((t_new, LANES), lambda b: (b, 0))],
        out_shape=[jax.ShapeDtypeStruct(o_prev.shape, BF16),
                   jax.ShapeDtypeStruct((n_seq * t_new, LANES), F32)],
        input_output_aliases={8: 0},
        compiler_params=_cparams(("parallel",)),
        name="fox_sample",
    )(p_all, p_all, p_all, p_all, bf_pad, cache_k, cache_v, cache_logf_t, o_prev)


def _merge_kernel(x_ref, oa_ref, ob_ref, ga_ref, gb_ref, pa_ref, pb_ref, wo_ref, o_ref):
    ma = jnp.dot(oa_ref[...], pa_ref[...], preferred_element_type=F32)
    mb = jnp.dot(ob_ref[...], pb_ref[...], preferred_element_type=F32)
    mix = _sigmoid(ga_ref[...]) * ma + _sigmoid(gb_ref[...]) * mb
    o_ref[...] = x_ref[...] + jnp.dot(mix.astype(BF16), wo_ref[...], preferred_element_type=F32)


def _merge(x, o_a, o_b, p_all, pa, pb, wo, tm):
    m, d = x.shape
    row = lambda n, col=0: pl.BlockSpec((tm, n), lambda i: (i, col))
    full = lambda a: pl.BlockSpec(a.shape, lambda i: (0, 0))
    return pl.pallas_call(
        _merge_kernel,
        grid=(m // tm,),
        in_specs=[row(d), row(W_H), row(W_H), row(d, COL_GA // d), row(d, COL_GB // d),
                  full(pa), full(pb), full(wo)],
        out_specs=row(d),
        out_shape=jax.ShapeDtypeStruct((m, d), F32),
        compiler_params=_cparams(("parallel",)),
        name="merge",
    )(x, o_a, o_b, p_all, p_all, pa, pb, wo)


def _ffn_kernel(x_ref, g_ref, wg_ref, wu_ref, wd_ref, fg_ref, o_ref, h_ref, acc_ref, *, final):
    f = pl.program_id(1)

    @pl.when(f == 0)
    def _():
        x = x_ref[...]
        y = x * lax.rsqrt(jnp.mean(x * x, axis=-1, keepdims=True) + EPS) * g_ref[...]
        h_ref[...] = y.astype(BF16)
        acc_ref[...] = x

    h = h_ref[...]
    gate = jnp.dot(h, wg_ref[...], preferred_element_type=F32)
    up = jnp.dot(h, wu_ref[...], preferred_element_type=F32)
    act = gate * _sigmoid(gate) * up
    acc_ref[...] += jnp.dot(act.astype(BF16), wd_ref[...], preferred_element_type=F32)

    @pl.when(f == pl.num_programs(1) - 1)
    def _():
        y = acc_ref[...]
        if final:
            y = y * lax.rsqrt(jnp.mean(y * y, axis=-1, keepdims=True) + EPS) * fg_ref[...]
        o_ref[...] = y


def _ffn(x, g, wg, wu, wd, fg, tm, tf, final):
    m, d = x.shape
    dff = wg.shape[1]
    return pl.pallas_call(
        functools.partial(_ffn_kernel, final=final),
        grid=(m // tm, dff // tf),
        in_specs=[pl.BlockSpec((tm, d), lambda i, f: (i, 0)),
                  pl.BlockSpec((1, d), lambda i, f: (0, 0)),
                  pl.BlockSpec((d, tf), lambda i, f: (0, f)),
                  pl.BlockSpec((d, tf), lambda i, f: (0, f)),
                  pl.BlockSpec((tf, d), lambda i, f: (f, 0)),
                  pl.BlockSpec((1, d), lambda i, f: (0, 0))],
        out_specs=pl.BlockSpec((tm, d), lambda i, f: (i, 0)),
        out_shape=jax.ShapeDtypeStruct((m, d), F32),
        scratch_shapes=[pltpu.VMEM((tm, d), BF16), pltpu.VMEM((tm, d), F32)],
        compiler_params=_cparams(("parallel", "arbitrary")),
        name="ffn",
    )(x, g, wg, wu, wd, fg)


def kernel(x_prompt, x_sample, cache_fox_k, cache_fox_v, cache_fox_logf, state_rwkv, state_shift,
           norm1_g, w_in, rwkv_mu, rwkv_w0, rwkv_w2, rwkv_a0, rwkv_a2, rwkv_g2, rwkv_k_k, rwkv_k_a,
           rwkv_r_k, rwkv_lnx_g, rwkv_lnx_b, fox_bf, p_a, p_b, w_out, norm2_g, w_gate, w_up, w_down,
           final_g):
    n_b, seq, d = x_prompt.shape
    n_db, t_new, _ = x_sample.shape
    depth = w_in.shape[0]
    past = cache_fox_k.shape[2]
    d_ff = w_gate.shape[2]
    m_prompt = n_b * seq
    m_all = m_prompt + n_db * t_new

    o_q = R_COLS
    o_fl = R_COLS + 3 * W_H
    o_g = o_fl + N_HEADS
    w_cat = jnp.concatenate(
        [w_in[:, :, o_g:o_g + 2 * d], w_in[:, :, o_q:o_q + 3 * W_H], w_in[:, :, 0:R_COLS],
         w_in[:, :, o_fl:o_fl + N_HEADS],
         jnp.zeros((depth, d, P_COLS - COL_FL - N_HEADS), w_in.dtype)], axis=-1).astype(BF16)
    bf_pad = jnp.pad(fox_bf, ((0, 0), (0, LANES - N_HEADS)))
    pa_b, pb_b, wo_b = p_a.astype(BF16), p_b.astype(BF16), w_out.astype(BF16)
    wg_b, wu_b, wd_b = w_gate.astype(BF16), w_up.astype(BF16), w_down.astype(BF16)
    w2_b, a2_b, g2_b = rwkv_w2.astype(BF16), rwkv_a2.astype(BF16), rwkv_g2.astype(BF16)
    cache_logf_t = jnp.swapaxes(cache_fox_logf, 2, 3)
    cache_k = cache_fox_k.reshape(depth, n_db, past, W_H)
    cache_v = cache_fox_v.reshape(depth, n_db, past, W_H)

    tm = _tile(m_all)
    tn = _tile(P_COLS)
    tf = _tile(d_ff, (1408, 256, 128))
    x = jnp.concatenate([x_prompt.reshape(m_prompt, d), x_sample.reshape(n_db * t_new, d)], axis=0)
    zero_shift = jnp.zeros((n_b, 1, R_COLS), F32)
    zero_state = jnp.zeros((n_b, N_HEADS, HEAD_DIM, HEAD_DIM), F32)

    outs = {k: [] for k in ('kp', 'vp', 'lfp', 'sp', 'shp', 'kd', 'vd', 'lfd', 'sd', 'shd')}
    for l in range(depth):
        row = lambda a: a[l].reshape(1, -1)
        lw = dict(mu=row(rwkv_mu), w0=row(rwkv_w0), w2=w2_b[l], a0=row(rwkv_a0), a2=a2_b[l],
                  g2=g2_b[l], k_k=row(rwkv_k_k), k_a=row(rwkv_k_a), r_k=row(rwkv_r_k),
                  lnx_g=row(rwkv_lnx_g), lnx_b=row(rwkv_lnx_b))
        p_all = _norm_matmul(x, row(norm1_g), w_cat[l], tm, tn)

        o_a, s_p, sh_p = _rwkv(p_all, None, zero_shift, zero_state, lw, row0=0, n_seq=n_b,
                               seq_len=seq, chunk=PROMPT_CHUNK)
        o_a, s_d, sh_d = _rwkv(p_all, o_a, state_shift[l], state_rwkv[l], lw, row0=m_prompt,
                               n_seq=n_db, seq_len=t_new, chunk=t_new)

        logf_p, c_p, ct_p = _logf_cumsum(p_all, bf_pad[l:l + 1], n_seq=n_b, seq_len=seq)
        o_b = _fox_prompt(p_all, c_p, ct_p, n_seq=n_b, seq_len=seq)
        o_b, logf_d = _fox_sample(p_all, bf_pad[l:l + 1], cache_k, cache_v, cache_logf_t,
                                  o_b, l, row0=m_prompt, n_seq=n_db, t_new=t_new)

        x = _merge(x, o_a, o_b, p_all, pa_b[l], pb_b[l], wo_b[l], tm)
        x = _ffn(x, row(norm2_g), wg_b[l], wu_b[l], wd_b[l], final_g.reshape(1, -1), tm, tf,
                 final=(l == depth - 1))

        heads = lambda a, nb, t: a.reshape(nb, t, N_HEADS, HEAD_DIM)
        outs['kp'].append(heads(p_all[:m_prompt, COL_K:COL_K + W_H], n_b, seq))
        outs['vp'].append(heads(p_all[:m_prompt, COL_V:COL_V + W_H], n_b, seq))
        outs['kd'].append(heads(p_all[m_prompt:, COL_K:COL_K + W_H], n_db, t_new))
        outs['vd'].append(heads(p_all[m_prompt:, COL_V:COL_V + W_H], n_db, t_new))
        outs['lfp'].append(logf_p[:, :N_HEADS].reshape(n_b, seq, N_HEADS))
        outs['lfd'].append(logf_d[:, :N_HEADS].reshape(n_db, t_new, N_HEADS))
        outs['sp'].append(s_p)
        outs['shp'].append(sh_p)
        outs['sd'].append(s_d)
        outs['shd'].append(sh_d)

    st = {k: jnp.stack(v) for k, v in outs.items()}
    y_prompt = x[:m_prompt].reshape(n_b, seq, d)
    y_sample = x[m_prompt:].reshape(n_db, t_new, d)
    return (y_prompt, y_sample, st['kp'], st['vp'], st['lfp'], st['sp'], st['shp'],
            st['kd'], st['vd'], st['lfd'], st['sd'], st['shd'])
```

```python
import functools

import numpy as np
import jax
import jax.numpy as jnp
from jax import lax
from jax.experimental import pallas as pl
from jax.experimental.pallas import tpu as pltpu

F32 = jnp.float32
BF16 = jnp.bfloat16

HEAD_DIM = 64
N_HEADS = 8
N_PAIRS = N_HEADS // 2
W_H = N_HEADS * HEAD_DIM
DECAY_LORA = 64
AAA_LORA = 64
GATE_LORA = 128
R_COLS = 3 * W_H + DECAY_LORA + AAA_LORA + GATE_LORA
EPS = 1e-6
GN_EPS = 64e-5
SCALE = HEAD_DIM ** -0.5

LANES = 128
VMEM_LIMIT_BYTES = 56 * 1024 * 1024

COL_GA = 0
COL_GB = 1024
COL_Q = 2048
COL_K = 2560
COL_V = 3072
COL_R = 3584
COL_FL = 5376
P_COLS = 5632
PROMPT_CHUNK = 64
ATT_BLOCK = 512
PREP_BLOCK = 512
KAUG_W = 2 * LANES
CK_LANE0 = 3


def _cparams(sem):
    return pltpu.CompilerParams(dimension_semantics=sem, vmem_limit_bytes=VMEM_LIMIT_BYTES)


def _mm(a, b):
    return jnp.dot(a.astype(BF16), b.astype(BF16), preferred_element_type=F32)


def _mm_nt(a, b):
    return lax.dot_general(a.astype(BF16), b.astype(BF16), (((1,), (1,)), ((), ())),
                           preferred_element_type=F32)


def _mm_tn(a, b):
    return lax.dot_general(a.astype(BF16), b.astype(BF16), (((0,), (0,)), ((), ())),
                           preferred_element_type=F32)


def _split3(x):
    hi = x.astype(BF16)
    r = x - hi.astype(F32)
    mid = r.astype(BF16)
    lo = (r - mid.astype(F32)).astype(BF16)
    return hi, mid, lo


def _mm_exact_rhs(m01, x):
    hi, mid, lo = _split3(x)
    d = lambda p: jnp.dot(m01, p, preferred_element_type=F32)
    return d(hi) + d(mid) + d(lo)


def _log_sigmoid(z):
    return jnp.minimum(z, 0.0) - jnp.log1p(jnp.exp(-jnp.abs(z)))


def _sigmoid(z):
    return 1.0 / (1.0 + jnp.exp(-z))


def _tile(m, cands=(512, 256, 128, 64, 32, 16, 8)):
    for t in cands:
        if m % t == 0:
            return t
    raise ValueError(m)


def _norm_matmul_kernel(x_ref, g_ref, w_ref, o_ref, h_ref):
    @pl.when(pl.program_id(1) == 0)
    def _():
        x = x_ref[...]
        y = x * lax.rsqrt(jnp.mean(x * x, axis=-1, keepdims=True) + EPS) * g_ref[...]
        h_ref[...] = y.astype(BF16)

    o_ref[...] = jnp.dot(h_ref[...], w_ref[...], preferred_element_type=F32)


def _norm_matmul(x, g, w, tm, tn):
    m, d = x.shape
    n = w.shape[1]
    return pl.pallas_call(
        _norm_matmul_kernel,
        grid=(m // tm, n // tn),
        in_specs=[pl.BlockSpec((tm, d), lambda i, j: (i, 0)),
                  pl.BlockSpec((1, d), lambda i, j: (0, 0)),
                  pl.BlockSpec((d, tn), lambda i, j: (0, j))],
        out_specs=pl.BlockSpec((tm, tn), lambda i, j: (i, j)),
        out_shape=jax.ShapeDtypeStruct((m, n), F32),
        scratch_shapes=[pltpu.VMEM((tm, d), BF16)],
        compiler_params=_cparams(("parallel", "arbitrary")),
        name="in_proj",
    )(x, g, w)


def _rwkv_kernel(pr_ref, shift_ref, s0_ref, mu_ref, w0_ref, w2_ref, a0_ref, a2_ref, g2_ref,
                 kk_ref, ka_ref, rk_ref, lg_ref, lb_ref,
                 o_ref, s_out_ref, shift_out_ref, state_ref, prev_ref, *, chunk, n_chunks):
    c = pl.program_id(1)
    C = chunk
    PW = LANES
    pairs = range(N_PAIRS)
    heads = [(p, hh) for p in pairs for hh in range(2)]

    @pl.when(c == 0)
    def _():
        state_ref[...] = s0_ref[0]
        prev_ref[...] = shift_ref[0]

    pr = pr_ref[...]
    row = lax.broadcasted_iota(jnp.int32, (C, 1), 0)
    shifted = jnp.where(row == 0, prev_ref[...], pltpu.roll(pr, 1, 0))
    u = pr + (shifted - pr) * mu_ref[...]
    prev_ref[...] = pr[C - 1:C, :]

    r = u[:, 0:W_H]
    k = u[:, W_H:2 * W_H]
    v = u[:, 2 * W_H:3 * W_H]
    wl = u[:, 3 * W_H:3 * W_H + DECAY_LORA]
    al = u[:, 3 * W_H + DECAY_LORA:3 * W_H + DECAY_LORA + AAA_LORA]
    gl = u[:, 3 * W_H + DECAY_LORA + AAA_LORA:R_COLS]

    w_raw = w0_ref[...] + _mm(jnp.tanh(wl), w2_ref[...])
    logw = -jnp.exp(_log_sigmoid(w_raw) - 0.5)
    a_sig = _sigmoid(a0_ref[...] + _mm(al, a2_ref[...]))
    gate = _mm(_sigmoid(gl), g2_ref[...])
    kk_all = k * kk_ref[...]
    k_new = k * (1.0 + (a_sig - 1.0) * ka_ref[...])

    ti = lax.broadcasted_iota(jnp.int32, (C, C), 0)
    si = lax.broadcasted_iota(jnp.int32, (C, C), 1)
    lower_incl = si <= ti
    lower_strict = si < ti
    g_cum = _mm_exact_rhs(lower_incl.astype(BF16), logw)
    e_incl = jnp.exp(g_cum)
    e_excl = jnp.exp(g_cum - logw)
    e_inv = jnp.exp(-g_cum)
    p_end = e_incl[C - 1:C, :]

    bi = lax.broadcasted_iota(jnp.int32, (PW, PW), 0) // HEAD_DIM
    bj = lax.broadcasted_iota(jnp.int32, (PW, PW), 1) // HEAD_DIM
    blockdiag = bi == bj
    ones_bd = blockdiag.astype(BF16)

    def pair_cols(x):
        return [x[:, p * PW:(p + 1) * PW] for p in pairs]

    def head_sums(xs):
        x = jnp.concatenate(xs, axis=0)
        hi = x.astype(BF16)
        lo = (x - hi.astype(F32)).astype(BF16)
        s = (jnp.dot(hi, ones_bd, preferred_element_type=F32)
             + jnp.dot(lo, ones_bd, preferred_element_type=F32))
        return [s[p * C:(p + 1) * C] for p in pairs]

    lane = lax.broadcasted_iota(jnp.int32, (1, PW), 1)
    in_h0 = lane < HEAD_DIM
    lane2 = lax.broadcasted_iota(jnp.int32, (1, 2 * PW), 1)
    in_h0_2 = (lane2 % PW) < HEAD_DIM

    r_p, k_p, v_p = pair_cols(r), pair_cols(k_new), pair_cols(v)
    kk_p = pair_cols(kk_all)
    asig_p = pair_cols(a_sig)
    ei_p, ee_p, ev_p, pc_p = pair_cols(e_incl), pair_cols(e_excl), pair_cols(e_inv), pair_cols(p_end)

    ss = head_sums([x * x for x in kk_p])
    kk_p = [kk_p[p] * lax.rsqrt(ss[p] + 1e-12) for p in pairs]
    a_t = [-kk_p[p] * ee_p[p] for p in pairs]
    r_t = [r_p[p] * ei_p[p] for p in pairs]
    b_hat = [kk_p[p] * asig_p[p] * ev_p[p] for p in pairs]
    k_hat = [k_p[p] * ev_p[p] for p in pairs]
    b_til = [b_hat[p] * pc_p[p] for p in pairs]
    k_til = [k_hat[p] * pc_p[p] for p in pairs]

    left = [jnp.concatenate([a_t[p], r_t[p]], axis=0) for p in pairs]
    n_pow, a_ak, a_rb, a_rk = {}, {}, {}, {}
    for (p, hh) in heads:
        lm = jnp.where(in_h0 if hh == 0 else ~in_h0, left[p], 0.0)
        ab = _mm_nt(lm, b_hat[p])
        ak = _mm_nt(lm, k_hat[p])
        n_pow[p, hh] = jnp.where(lower_strict, ab[:C], 0.0)
        a_ak[p, hh] = jnp.where(lower_strict, ak[:C], 0.0)
        a_rb[p, hh] = jnp.where(lower_incl, ab[C:], 0.0)
        a_rk[p, hh] = jnp.where(lower_incl, ak[C:], 0.0)

    def by_head(mats, x, mask):
        return jnp.where(mask, _mm(mats[0], x), _mm(mats[1], x))

    xs = [jnp.concatenate([a_t[p], by_head((a_ak[p, 0], a_ak[p, 1]), v_p[p], in_h0)], axis=1)
          for p in pairs]
    n_levels = C.bit_length() - 1
    for lvl in range(n_levels):
        xs = [xs[p] + by_head((n_pow[p, 0], n_pow[p, 1]), xs[p], in_h0_2) for p in pairs]
        if lvl + 1 < n_levels:
            n_pow = {hd: _mm(n_pow[hd], n_pow[hd]) for hd in heads}

    s_prev = [state_ref[p] for p in pairs]
    u_p = [_mm_nt(xs[p][:, :PW], s_prev[p]) + xs[p][:, PW:] for p in pairs]
    y_p = [_mm_nt(r_t[p], s_prev[p])
           + by_head((a_rb[p, 0], a_rb[p, 1]), u_p[p], in_h0)
           + by_head((a_rk[p, 0], a_rk[p, 1]), v_p[p], in_h0) for p in pairs]
    for p in pairs:
        upd = _mm_tn(jnp.concatenate([u_p[p], v_p[p]], axis=0),
                     jnp.concatenate([b_til[p], k_til[p]], axis=0))
        state_ref[p] = s_prev[p] * pc_p[p] + jnp.where(blockdiag, upd, 0.0)

    inv_n = 1.0 / HEAD_DIM
    mu_y = head_sums(y_p)
    yc = [y_p[p] - mu_y[p] * inv_n for p in pairs]
    var = head_sums([x * x for x in yc])
    rk_p = pair_cols(rk_ref[...])
    bon = head_sums([r_p[p] * k_p[p] * rk_p[p] for p in pairs])
    for p in pairs:
        cols = slice(p * PW, (p + 1) * PW)
        yn = yc[p] * lax.rsqrt(var[p] * inv_n + GN_EPS) * lg_ref[:, cols] + lb_ref[:, cols]
        o_ref[:, cols] = ((yn + bon[p] * v_p[p]) * gate[:, cols]).astype(o_ref.dtype)

    @pl.when(c == n_chunks - 1)
    def _():
        for (p, hh) in heads:
            lo = hh * HEAD_DIM
            s_out_ref[0, 2 * p + hh] = state_ref[p, lo:lo + HEAD_DIM, lo:lo + HEAD_DIM]
        shift_out_ref[0] = pr[C - 1:C, :]


def _rwkv(p_all, o_prev, shift_in, s0_bd, lw, *, row0, n_seq, seq_len, chunk):
    n_chunks = seq_len // chunk
    rb0 = row0 // chunk
    vec = lambda n: pl.BlockSpec((1, n), lambda b, c: (0, 0))
    mat = lambda k, n: pl.BlockSpec((k, n), lambda b, c: (0, 0))
    m = p_all.shape[0]
    kern = functools.partial(_rwkv_kernel, chunk=chunk, n_chunks=n_chunks)
    in_specs = [
        pl.BlockSpec((chunk, R_COLS), lambda b, c: (rb0 + b * n_chunks + c, COL_R // R_COLS)),
        pl.BlockSpec((1, 1, R_COLS), lambda b, c: (b, 0, 0)),
        pl.BlockSpec((1, N_PAIRS, LANES, LANES), lambda b, c: (b, 0, 0, 0)),
        vec(R_COLS), vec(W_H), mat(DECAY_LORA, W_H), vec(W_H), mat(AAA_LORA, W_H),
        mat(GATE_LORA, W_H), vec(W_H), vec(W_H), vec(W_H), vec(W_H), vec(W_H),
    ]
    args = [p_all, shift_in, s0_bd, lw['mu'], lw['w0'], lw['w2'], lw['a0'], lw['a2'], lw['g2'],
            lw['k_k'], lw['k_a'], lw['r_k'], lw['lnx_g'], lw['lnx_b']]
    aliases = {}
    if o_prev is not None:
        in_specs.append(pl.BlockSpec(memory_space=pl.ANY))
        args.append(o_prev)
        aliases = {len(args) - 1: 0}
        kern_fn = lambda *refs: kern(*refs[:14], *refs[15:])
    else:
        kern_fn = kern
    return pl.pallas_call(
        kern_fn,
        grid=(n_seq, n_chunks),
        in_specs=in_specs,
        out_specs=[pl.BlockSpec((chunk, W_H), lambda b, c: (rb0 + b * n_chunks + c, 0)),
                   pl.BlockSpec((1, N_HEADS, HEAD_DIM, HEAD_DIM), lambda b, c: (b, 0, 0, 0)),
                   pl.BlockSpec((1, 1, R_COLS), lambda b, c: (b, 0, 0))],
        out_shape=[jax.ShapeDtypeStruct((m, W_H), BF16),
                   jax.ShapeDtypeStruct((n_seq, N_HEADS, HEAD_DIM, HEAD_DIM), F32),
                   jax.ShapeDtypeStruct((n_seq, 1, R_COLS), F32)],
        scratch_shapes=[pltpu.VMEM((N_PAIRS, LANES, LANES), F32),
                        pltpu.VMEM((1, R_COLS), F32)],
        input_output_aliases=aliases,
        compiler_params=_cparams(("parallel", "arbitrary")),
        name=f"rwkv_c{chunk}",
    )(*args)


def _pair_block_diag(s):
    n = s.shape[0]
    s5 = s.reshape(n, N_PAIRS, 2, HEAD_DIM, HEAD_DIM)
    eye2 = jnp.eye(2, dtype=s.dtype)
    bd = s5[:, :, :, :, None, :] * eye2[None, None, :, None, :, None]
    return bd.reshape(n, N_PAIRS, LANES, LANES)


def _place_matrix():
    pm = np.zeros((3 * LANES, W_H), np.float32)
    for piece in range(3):
        for h in range(N_HEADS):
            pm[piece * LANES + h, (h // 2) * LANES + CK_LANE0 + 3 * (h % 2) + piece] = 1.0
    return jnp.asarray(pm, BF16)


def _fox_prep_kernel(k_ref, v_ref, fl_ref, bf_ref, pm_ref, *rest, has_prev):
    if has_prev:
        rest = rest[2:]
    kall_ref, vall_ref, logf_ref, c_ref, kaug_ref, vbf_ref, carry_ref = rest

    @pl.when(pl.program_id(1) == 0)
    def _():
        carry_ref[...] = jnp.zeros_like(carry_ref)

    tc = fl_ref.shape[0]
    k = k_ref[...]
    v = v_ref[...]
    kall_ref[0] = k
    vall_ref[0] = v
    vbf_ref[...] = v.astype(BF16)
    logf = _log_sigmoid(fl_ref[...] + bf_ref[...])
    logf_ref[...] = logf
    ti = lax.broadcasted_iota(jnp.int32, (tc, tc), 0)
    si = lax.broadcasted_iota(jnp.int32, (tc, tc), 1)
    c = _mm_exact_rhs((si <= ti).astype(BF16), logf) + carry_ref[...]
    c_ref[...] = c
    carry_ref[...] = c[tc - 1:tc, :]
    pieces = jnp.concatenate(_split3(-c), axis=1)
    kc = jnp.dot(pieces, pm_ref[...], preferred_element_type=F32)
    lane = lax.broadcasted_iota(jnp.int32, (1, LANES), 1)
    ones3 = (lane < CK_LANE0).astype(F32)
    for p in range(N_PAIRS):
        cols = slice(p * LANES, (p + 1) * LANES)
        kaug_ref[:, p * KAUG_W:p * KAUG_W + LANES] = k[:, cols].astype(BF16)
        kaug_ref[:, p * KAUG_W + LANES:(p + 1) * KAUG_W] = (kc[:, cols] + ones3).astype(BF16)


def _fox_prep(p_all, bf_pad, pm, k_prev, v_prev, layer, depth, *, n_seq, seq_len):
    tc = _tile(seq_len, (PREP_BLOCK, 256, 128))
    nb = seq_len // tc
    rows = n_seq * seq_len
    has_prev = k_prev is not None
    in_specs = [pl.BlockSpec((tc, W_H), lambda b, j: (b * nb + j, COL_K // W_H)),
                pl.BlockSpec((tc, W_H), lambda b, j: (b * nb + j, COL_V // W_H)),
                pl.BlockSpec((tc, LANES), lambda b, j: (b * nb + j, COL_FL // LANES)),
                pl.BlockSpec((1, LANES), lambda b, j: (0, 0)),
                pl.BlockSpec((3 * LANES, W_H), lambda b, j: (0, 0))]
    args = [p_all, p_all, p_all, bf_pad, pm]
    aliases = {}
    if has_prev:
        in_specs += [pl.BlockSpec(memory_space=pl.ANY), pl.BlockSpec(memory_space=pl.ANY)]
        args += [k_prev, v_prev]
        aliases = {5: 0, 6: 1}
    rowblk = lambda n: pl.BlockSpec((tc, n), lambda b, j: (b * nb + j, 0))
    stacked = pl.BlockSpec((1, tc, W_H), lambda b, j: (layer, b * nb + j, 0))
    return pl.pallas_call(
        functools.partial(_fox_prep_kernel, has_prev=has_prev),
        grid=(n_seq, nb),
        in_specs=in_specs,
        out_specs=[stacked, stacked, rowblk(LANES), rowblk(LANES), rowblk(N_PAIRS * KAUG_W),
                   rowblk(W_H)],
        out_shape=[jax.ShapeDtypeStruct((depth, rows, W_H), F32),
                   jax.ShapeDtypeStruct((depth, rows, W_H), F32),
                   jax.ShapeDtypeStruct((rows, LANES), F32),
                   jax.ShapeDtypeStruct((rows, LANES), F32),
                   jax.ShapeDtypeStruct((rows, N_PAIRS * KAUG_W), BF16),
                   jax.ShapeDtypeStruct((rows, W_H), BF16)],
        scratch_shapes=[pltpu.VMEM((1, LANES), F32)],
        input_output_aliases=aliases,
        compiler_params=_cparams(("parallel", "arbitrary")),
        name="fox_prep",
    )(*args)


def _fox_prompt_kernel(q_ref, c_ref, kaug_ref, v_ref, o_ref, *, blk):
    hp = pl.program_id(1)
    i = pl.program_id(2)
    lane = lax.broadcasted_iota(jnp.int32, (1, LANES), 1)
    rowi = lax.broadcasted_iota(jnp.int32, (blk, blk), 0)
    coli = lax.broadcasted_iota(jnp.int32, (blk, blk), 1)
    causal = coli <= rowi
    c_blk = c_ref[...]
    q_all = q_ref[...] * SCALE
    q_aug = []
    for hh in range(2):
        in_head = (lane < HEAD_DIM) if hh == 0 else (lane >= HEAD_DIM)
        qm = jnp.where(in_head, q_all, 0.0).astype(BF16)
        cq = jnp.sum(jnp.where(lane == 2 * hp + hh, c_blk, 0.0), axis=-1, keepdims=True)
        hi, mid, lo = _split3(cq)
        ck0 = CK_LANE0 + 3 * hh
        qc = jnp.where(lane == 0, hi.astype(F32),
                       jnp.where(lane == 1, mid.astype(F32),
                                 jnp.where(lane == 2, lo.astype(F32),
                                           jnp.where((lane >= ck0) & (lane < ck0 + 3), 1.0, 0.0))))
        q_aug.append(jnp.concatenate([qm, qc.astype(BF16)], axis=1))

    def step(j, carry, masked):
        start = pl.multiple_of(j * blk, blk)
        kb = kaug_ref[pl.ds(start, blk), :]
        vb = v_ref[pl.ds(start, blk), :]
        new = []
        for hh in range(2):
            m, l, acc = carry[hh]
            s = lax.dot_general(q_aug[hh], kb, (((1,), (1,)), ((), ())),
                                preferred_element_type=F32)
            if masked:
                s = jnp.where(causal, s, -jnp.inf)
            m_new = jnp.maximum(m, jnp.max(s, axis=-1, keepdims=True))
            alpha = jnp.exp(m - m_new)
            p = jnp.exp(s - m_new)
            l = alpha * l + jnp.sum(p, axis=-1, keepdims=True)
            acc = alpha * acc + jnp.dot(p.astype(BF16), vb, preferred_element_type=F32)
            new.append((m_new, l, acc))
        return tuple(new)

    init = tuple((jnp.full((blk, 1), -jnp.inf, F32), jnp.zeros((blk, 1), F32),
                  jnp.zeros((blk, LANES), F32)) for _ in range(2))
    carry = lax.fori_loop(0, i, lambda j, cr: step(j, cr, False), init)
    (_, l0, acc0), (_, l1, acc1) = step(i, carry, True)
    o_ref[...] = jnp.where(lane < HEAD_DIM, acc0 / l0, acc1 / l1).astype(o_ref.dtype)


def _fox_prompt(p_all, c, kaug, v_bf, *, n_seq, seq_len):
    blk = _tile(seq_len, (ATT_BLOCK, 256, 128))
    nq = seq_len // blk
    m = p_all.shape[0]
    kern = functools.partial(_fox_prompt_kernel, blk=blk)
    return pl.pallas_call(
        kern,
        grid=(n_seq, N_PAIRS, nq),
        in_specs=[pl.BlockSpec((blk, LANES), lambda b, hp, i: (b * nq + i, COL_Q // LANES + hp)),
                  pl.BlockSpec((blk, LANES), lambda b, hp, i: (b * nq + i, 0)),
                  pl.BlockSpec((seq_len, KAUG_W), lambda b, hp, i: (b, hp)),
                  pl.BlockSpec((seq_len, LANES), lambda b, hp, i: (b, hp))],
        out_specs=pl.BlockSpec((blk, LANES), lambda b, hp, i: (b * nq + i, hp)),
        out_shape=jax.ShapeDtypeStruct((m, W_H), BF16),
        compiler_params=_cparams(("parallel", "parallel", "arbitrary")),
        name="fox_prompt",
    )(p_all, c, kaug, v_bf)


def _fox_sample_kernel(q_ref, k_ref, v_ref, fl_ref, bf_ref, ck_ref, cv_ref, clt_ref, *rest,
                       t_new, past, n_alias):
    rest = rest[n_alias:]
    o_ref, logf_ref, kall_ref, vall_ref = rest
    T = t_new
    kall_ref[0] = k_ref[...]
    vall_ref[0] = v_ref[...]
    logf = _log_sigmoid(fl_ref[...] + bf_ref[...])
    logf_ref[...] = logf
    logf_pad = jnp.concatenate([logf, jnp.zeros((LANES - T, LANES), F32)], axis=0)
    ti = lax.broadcasted_iota(jnp.int32, (LANES, LANES), 0)
    si = lax.broadcasted_iota(jnp.int32, (LANES, LANES), 1)
    cnew_pad = _mm_exact_rhs((si <= ti).astype(BF16), logf_pad)
    cnew_t = cnew_pad.T
    cnew = cnew_pad[0:T, :]
    suf = clt_ref[0, 0]
    pos = lax.broadcasted_iota(jnp.int32, suf.shape, 1)
    total = suf
    d = 1
    while d < past:
        total = total + jnp.where(pos + d < past, pltpu.roll(total, past - d, 1), 0.0)
        d *= 2
    suf = total - suf

    lane = lax.broadcasted_iota(jnp.int32, (T, LANES), 1)
    rowi = lax.broadcasted_iota(jnp.int32, (T, T), 0)
    coli = lax.broadcasted_iota(jnp.int32, (T, T), 1)
    for h in range(N_HEADS):
        sl = slice(h * HEAD_DIM, (h + 1) * HEAD_DIM)
        q = q_ref[:, sl] * SCALE
        cq = jnp.sum(jnp.where(lane == h, cnew, 0.0), axis=-1, keepdims=True)
        s_c = _mm_nt(q, ck_ref[0, 0, :, sl]) + (cq + suf[h:h + 1, :])
        s_n = _mm_nt(q, k_ref[:, sl]) + (cq - cnew_t[h:h + 1, 0:T])
        s_n = jnp.where(coli <= rowi, s_n, -jnp.inf)
        m = jnp.maximum(jnp.max(s_c, axis=-1, keepdims=True), jnp.max(s_n, axis=-1, keepdims=True))
        p_c = jnp.exp(s_c - m)
        p_n = jnp.exp(s_n - m)
        l = jnp.sum(p_c, axis=-1, keepdims=True) + jnp.sum(p_n, axis=-1, keepdims=True)
        o = _mm(p_c, cv_ref[0, 0, :, sl]) + _mm(p_n, v_ref[:, sl])
        o_ref[:, sl] = (o / l).astype(o_ref.dtype)


def _fox_sample(p_all, bf_pad, cache_k, cache_v, cache_logf_t, o_prev, k_prev, v_prev, layer, depth,
                *, row0, n_seq, t_new):
    past = cache_k.shape[2]
    rb0 = row0 // t_new
    blk512 = lambda col: pl.BlockSpec((t_new, W_H), lambda b: (rb0 + b, col // W_H))
    in_specs = [blk512(COL_Q), blk512(COL_K), blk512(COL_V),
                pl.BlockSpec((t_new, LANES), lambda b: (rb0 + b, COL_FL // LANES)),
                pl.BlockSpec((1, LANES), lambda b: (0, 0)),
                pl.BlockSpec((1, 1, past, W_H), lambda b: (layer, b, 0, 0)),
                pl.BlockSpec((1, 1, past, W_H), lambda b: (layer, b, 0, 0)),
                pl.BlockSpec((1, 1, N_HEADS, past), lambda b: (layer, b, 0, 0)),
                pl.BlockSpec(memory_space=pl.ANY)]
    args = [p_all, p_all, p_all, p_all, bf_pad, cache_k, cache_v, cache_logf_t, o_prev]
    aliases = {8: 0}
    if k_prev is not None:
        in_specs += [pl.BlockSpec(memory_space=pl.ANY), pl.BlockSpec(memory_space=pl.ANY)]
        args += [k_prev, v_prev]
        aliases.update({9: 2, 10: 3})
    kern = functools.partial(_fox_sample_kernel, t_new=t_new, past=past, n_alias=len(aliases))
    stacked = pl.BlockSpec((1, t_new, W_H), lambda b: (layer, b, 0))
    return pl.pallas_call(
        kern,
        grid=(n_seq,),
        in_specs=in_specs,
        out_specs=[pl.BlockSpec((t_new, W_H), lambda b: (rb0 + b, 0)),
                   pl.BlockSpec((t_new, LANES), lambda b: (b, 0)), stacked, stacked],
        out_shape=[jax.ShapeDtypeStruct(o_prev.shape, BF16),
                   jax.ShapeDtypeStruct((n_seq * t_new, LANES), F32),
                   jax.ShapeDtypeStruct((depth, n_seq * t_new, W_H), F32),
                   jax.ShapeDtypeStruct((depth, n_seq * t_new, W_H), F32)],
        input_output_aliases=aliases,
        compiler_params=_cparams(("parallel",)),
        name="fox_sample",
    )(*args)


def _merge_kernel(x_ref, oa_ref, ob_ref, ga_ref, gb_ref, pa_ref, pb_ref, wo_ref, o_ref):
    ma = jnp.dot(oa_ref[...], pa_ref[...], preferred_element_type=F32)
    mb = jnp.dot(ob_ref[...], pb_ref[...], preferred_element_type=F32)
    mix = _sigmoid(ga_ref[...]) * ma + _sigmoid(gb_ref[...]) * mb
    o_ref[...] = x_ref[...] + jnp.dot(mix.astype(BF16), wo_ref[...], preferred_element_type=F32)


def _merge(x, o_a, o_b, p_all, pa, pb, wo, tm):
    m, d = x.shape
    row = lambda n, col=0: pl.BlockSpec((tm, n), lambda i: (i, col))
    full = lambda a: pl.BlockSpec(a.shape, lambda i: (0, 0))
    return pl.pallas_call(
        _merge_kernel,
        grid=(m // tm,),
        in_specs=[row(d), row(W_H), row(W_H), row(d, COL_GA // d), row(d, COL_GB // d),
                  full(pa), full(pb), full(wo)],
        out_specs=row(d),
        out_shape=jax.ShapeDtypeStruct((m, d), F32),
        compiler_params=_cparams(("parallel",)),
        name="merge",
    )(x, o_a, o_b, p_all, p_all, pa, pb, wo)


def _ffn_kernel(x_ref, g_ref, wg_ref, wu_ref, wd_ref, fg_ref, o_ref, h_ref, acc_ref, *, final):
    f = pl.program_id(1)

    @pl.when(f == 0)
    def _():
        x = x_ref[...]
        y = x * lax.rsqrt(jnp.mean(x * x, axis=-1, keepdims=True) + EPS) * g_ref[...]
        h_ref[...] = y.astype(BF16)
        acc_ref[...] = x

    h = h_ref[...]
    gate = jnp.dot(h, wg_ref[...], preferred_element_type=F32)
    up = jnp.dot(h, wu_ref[...], preferred_element_type=F32)
    act = gate * _sigmoid(gate) * up
    acc_ref[...] += jnp.dot(act.astype(BF16), wd_ref[...], preferred_element_type=F32)

    @pl.when(f == pl.num_programs(1) - 1)
    def _():
        y = acc_ref[...]
        if final:
            y = y * lax.rsqrt(jnp.mean(y * y, axis=-1, keepdims=True) + EPS) * fg_ref[...]
        o_ref[...] = y


def _ffn(x, g, wg, wu, wd, fg, tm, tf, final):
    m, d = x.shape
    dff = wg.shape[1]
    return pl.pallas_call(
        functools.partial(_ffn_kernel, final=final),
        grid=(m // tm, dff // tf),
        in_specs=[pl.BlockSpec((tm, d), lambda i, f: (i, 0)),
                  pl.BlockSpec((1, d), lambda i, f: (0, 0)),
                  pl.BlockSpec((d, tf), lambda i, f: (0, f)),
                  pl.BlockSpec((d, tf), lambda i, f: (0, f)),
                  pl.BlockSpec((tf, d), lambda i, f: (f, 0)),
                  pl.BlockSpec((1, d), lambda i, f: (0, 0))],
        out_specs=pl.BlockSpec((tm, d), lambda i, f: (i, 0)),
        out_shape=jax.ShapeDtypeStruct((m, d), F32),
        scratch_shapes=[pltpu.VMEM((tm, d), BF16), pltpu.VMEM((tm, d), F32)],
        compiler_params=_cparams(("parallel", "arbitrary")),
        name="ffn",
    )(x, g, wg, wu, wd, fg)


def kernel(x_prompt, x_sample, cache_fox_k, cache_fox_v, cache_fox_logf, state_rwkv, state_shift,
           norm1_g, w_in, rwkv_mu, rwkv_w0, rwkv_w2, rwkv_a0, rwkv_a2, rwkv_g2, rwkv_k_k, rwkv_k_a,
           rwkv_r_k, rwkv_lnx_g, rwkv_lnx_b, fox_bf, p_a, p_b, w_out, norm2_g, w_gate, w_up, w_down,
           final_g):
    n_b, seq, d = x_prompt.shape
    n_db, t_new, _ = x_sample.shape
    depth = w_in.shape[0]
    past = cache_fox_k.shape[2]
    d_ff = w_gate.shape[2]
    m_prompt = n_b * seq
    m_all = m_prompt + n_db * t_new

    o_q = R_COLS
    o_fl = R_COLS + 3 * W_H
    o_g = o_fl + N_HEADS
    w_cat = jnp.concatenate(
        [w_in[:, :, o_g:o_g + 2 * d], w_in[:, :, o_q:o_q + 3 * W_H], w_in[:, :, 0:R_COLS],
         w_in[:, :, o_fl:o_fl + N_HEADS],
         jnp.zeros((depth, d, P_COLS - COL_FL - N_HEADS), w_in.dtype)], axis=-1).astype(BF16)
    bf_pad = jnp.pad(fox_bf, ((0, 0), (0, LANES - N_HEADS)))
    pa_b, pb_b, wo_b = p_a.astype(BF16), p_b.astype(BF16), w_out.astype(BF16)
    wg_b, wu_b, wd_b = w_gate.astype(BF16), w_up.astype(BF16), w_down.astype(BF16)
    w2_b, a2_b, g2_b = rwkv_w2.astype(BF16), rwkv_a2.astype(BF16), rwkv_g2.astype(BF16)
    cache_logf_t = jnp.swapaxes(cache_fox_logf, 2, 3)
    cache_k = cache_fox_k.reshape(depth, n_db, past, W_H)
    cache_v = cache_fox_v.reshape(depth, n_db, past, W_H)
    pm = _place_matrix()

    tm = _tile(m_all)
    tn = _tile(P_COLS, (2816, 512))
    tf = _tile(d_ff, (1408, 256, 128))
    x = jnp.concatenate([x_prompt.reshape(m_prompt, d), x_sample.reshape(n_db * t_new, d)], axis=0)
    zero_shift = jnp.zeros((n_b, 1, R_COLS), F32)
    zero_state = jnp.zeros((n_b, N_PAIRS, LANES, LANES), F32)

    outs = {k: [] for k in ('lfp', 'sp', 'shp', 'lfd', 'sd', 'shd')}
    kp = vp = kd = vd = None
    for l in range(depth):
        row = lambda a: a[l].reshape(1, -1)
        lw = dict(mu=row(rwkv_mu), w0=row(rwkv_w0), w2=w2_b[l], a0=row(rwkv_a0), a2=a2_b[l],
                  g2=g2_b[l], k_k=row(rwkv_k_k), k_a=row(rwkv_k_a), r_k=row(rwkv_r_k),
                  lnx_g=row(rwkv_lnx_g), lnx_b=row(rwkv_lnx_b))
        p_all = _norm_matmul(x, row(norm1_g), w_cat[l], tm, tn)

        o_a, s_p, sh_p = _rwkv(p_all, None, zero_shift, zero_state, lw, row0=0, n_seq=n_b,
                               seq_len=seq, chunk=PROMPT_CHUNK)
        o_a, s_d, sh_d = _rwkv(p_all, o_a, state_shift[l], _pair_block_diag(state_rwkv[l]), lw,
                               row0=m_prompt, n_seq=n_db, seq_len=t_new, chunk=t_new)

        kp, vp, logf_p, c_p, kaug, v_bf = _fox_prep(p_all, bf_pad[l:l + 1], pm, kp, vp, l, depth,
                                                    n_seq=n_b, seq_len=seq)
        o_b = _fox_prompt(p_all, c_p, kaug, v_bf, n_seq=n_b, seq_len=seq)
        o_b, logf_d, kd, vd = _fox_sample(p_all, bf_pad[l:l + 1], cache_k, cache_v, cache_logf_t,
                                          o_b, kd, vd, l, depth, row0=m_prompt, n_seq=n_db,
                                          t_new=t_new)

        x = _merge(x, o_a, o_b, p_all, pa_b[l], pb_b[l], wo_b[l], tm)
        x = _ffn(x, row(norm2_g), wg_b[l], wu_b[l], wd_b[l], final_g.reshape(1, -1), tm, tf,
                 final=(l == depth - 1))

        outs['lfp'].append(logf_p[:, :N_HEADS].reshape(n_b, seq, N_HEADS))
        outs['lfd'].append(logf_d[:, :N_HEADS].reshape(n_db, t_new, N_HEADS))
        outs['sp'].append(s_p)
        outs['shp'].append(sh_p)
        outs['sd'].append(s_d)
        outs['shd'].append(sh_d)

    st = {k: jnp.stack(v) for k, v in outs.items()}
    heads = lambda a, nb, t: a.reshape(depth, nb, t, N_HEADS, HEAD_DIM)
    y_prompt = x[:m_prompt].reshape(n_b, seq, d)
    y_sample = x[m_prompt:].reshape(n_db, t_new, d)
    return (y_prompt, y_sample, heads(kp, n_b, seq), heads(vp, n_b, seq), st['lfp'], st['sp'],
            st['shp'], heads(kd, n_db, t_new), heads(vd, n_db, t_new), st['lfd'], st['sd'],
            st['shd'])
```

```python
import functools

import numpy as np
import jax
import jax.numpy as jnp
from jax import lax
from jax.experimental import pallas as pl
from jax.experimental.pallas import tpu as pltpu

F32 = jnp.float32
BF16 = jnp.bfloat16

HEAD_DIM = 64
N_HEADS = 8
N_PAIRS = N_HEADS // 2
W_H = N_HEADS * HEAD_DIM
DECAY_LORA = 64
AAA_LORA = 64
GATE_LORA = 128
R_COLS = 3 * W_H + DECAY_LORA + AAA_LORA + GATE_LORA
EPS = 1e-6
GN_EPS = 64e-5
SCALE = HEAD_DIM ** -0.5

LANES = 128
VMEM_LIMIT_BYTES = 56 * 1024 * 1024

COL_GA = 0
COL_GB = 1024
COL_Q = 2048
COL_K = 2560
COL_V = 3072
COL_R = 3584
COL_FL = 5376
P_COLS = 5632
PROMPT_CHUNK = 64
ATT_BLOCK = 512
PREP_BLOCK = 512
KAUG_W = 2 * LANES
CQ_LANE0 = HEAD_DIM
CK_LANE0 = HEAD_DIM + 3
LOG2E = 1.4426950408889634


def _cparams(sem):
    return pltpu.CompilerParams(dimension_semantics=sem, vmem_limit_bytes=VMEM_LIMIT_BYTES)


def _mm(a, b):
    return jnp.dot(a.astype(BF16), b.astype(BF16), preferred_element_type=F32)


def _mm_nt(a, b):
    return lax.dot_general(a.astype(BF16), b.astype(BF16), (((1,), (1,)), ((), ())),
                           preferred_element_type=F32)


def _mm_tn(a, b):
    return lax.dot_general(a.astype(BF16), b.astype(BF16), (((0,), (0,)), ((), ())),
                           preferred_element_type=F32)


def _split3(x):
    hi = x.astype(BF16)
    r = x - hi.astype(F32)
    mid = r.astype(BF16)
    lo = (r - mid.astype(F32)).astype(BF16)
    return hi, mid, lo


def _mm_exact_rhs(m01, x):
    hi, mid, lo = _split3(x)
    d = lambda p: jnp.dot(m01, p, preferred_element_type=F32)
    return d(hi) + d(mid) + d(lo)


def _log_sigmoid(z):
    return jnp.minimum(z, 0.0) - jnp.log1p(jnp.exp(-jnp.abs(z)))


def _sigmoid(z):
    return 1.0 / (1.0 + jnp.exp(-z))


def _tile(m, cands=(512, 256, 128, 64, 32, 16, 8)):
    for t in cands:
        if m % t == 0:
            return t
    raise ValueError(m)


def _norm_matmul_kernel(x_ref, g_ref, w_ref, o_ref, h_ref):
    @pl.when(pl.program_id(1) == 0)
    def _():
        x = x_ref[...]
        y = x * lax.rsqrt(jnp.mean(x * x, axis=-1, keepdims=True) + EPS) * g_ref[...]
        h_ref[...] = y.astype(BF16)

    o_ref[...] = jnp.dot(h_ref[...], w_ref[...], preferred_element_type=F32)


def _norm_matmul(x, g, w, tm, tn):
    m, d = x.shape
    n = w.shape[1]
    return pl.pallas_call(
        _norm_matmul_kernel,
        grid=(m // tm, n // tn),
        in_specs=[pl.BlockSpec((tm, d), lambda i, j: (i, 0)),
                  pl.BlockSpec((1, d), lambda i, j: (0, 0)),
                  pl.BlockSpec((d, tn), lambda i, j: (0, j))],
        out_specs=pl.BlockSpec((tm, tn), lambda i, j: (i, j)),
        out_shape=jax.ShapeDtypeStruct((m, n), F32),
        scratch_shapes=[pltpu.VMEM((tm, d), BF16)],
        compiler_params=_cparams(("parallel", "arbitrary")),
        name="in_proj",
    )(x, g, w)


def _rwkv_kernel(pr_ref, shift_ref, s0_ref, mu_ref, w0_ref, w2_ref, a0_ref, a2_ref, g2_ref,
                 kk_ref, ka_ref, rk_ref, lg_ref, lb_ref,
                 o_ref, s_out_ref, shift_out_ref, state_ref, prev_ref, *, chunk, n_chunks):
    c = pl.program_id(1)
    C = chunk
    PW = LANES
    pairs = range(N_PAIRS)
    heads = [(p, hh) for p in pairs for hh in range(2)]

    @pl.when(c == 0)
    def _():
        state_ref[...] = s0_ref[0]
        prev_ref[...] = shift_ref[0]

    pr = pr_ref[...]
    row = lax.broadcasted_iota(jnp.int32, (C, 1), 0)
    shifted = jnp.where(row == 0, prev_ref[...], pltpu.roll(pr, 1, 0))
    u = pr + (shifted - pr) * mu_ref[...]
    prev_ref[...] = pr[C - 1:C, :]

    r = u[:, 0:W_H]
    k = u[:, W_H:2 * W_H]
    v = u[:, 2 * W_H:3 * W_H]
    wl = u[:, 3 * W_H:3 * W_H + DECAY_LORA]
    al = u[:, 3 * W_H + DECAY_LORA:3 * W_H + DECAY_LORA + AAA_LORA]
    gl = u[:, 3 * W_H + DECAY_LORA + AAA_LORA:R_COLS]

    w_raw = w0_ref[...] + _mm(jnp.tanh(wl), w2_ref[...])
    logw = -jnp.exp(_log_sigmoid(w_raw) - 0.5)
    a_sig = _sigmoid(a0_ref[...] + _mm(al, a2_ref[...]))
    gate = _mm(_sigmoid(gl), g2_ref[...])
    kk_all = k * kk_ref[...]
    k_new = k * (1.0 + (a_sig - 1.0) * ka_ref[...])

    ti = lax.broadcasted_iota(jnp.int32, (C, C), 0)
    si = lax.broadcasted_iota(jnp.int32, (C, C), 1)
    lower_incl = si <= ti
    lower_strict = si < ti
    g_cum = _mm_exact_rhs(lower_incl.astype(BF16), logw)
    e_incl = jnp.exp(g_cum)
    e_excl = jnp.exp(g_cum - logw)
    e_inv = jnp.exp(-g_cum)
    p_end = e_incl[C - 1:C, :]

    bi = lax.broadcasted_iota(jnp.int32, (PW, PW), 0) // HEAD_DIM
    bj = lax.broadcasted_iota(jnp.int32, (PW, PW), 1) // HEAD_DIM
    blockdiag = bi == bj
    ones_bd = blockdiag.astype(BF16)

    def pair_cols(x):
        return [x[:, p * PW:(p + 1) * PW] for p in pairs]

    def head_sums(xs):
        x = jnp.concatenate(xs, axis=0)
        hi = x.astype(BF16)
        lo = (x - hi.astype(F32)).astype(BF16)
        s = (jnp.dot(hi, ones_bd, preferred_element_type=F32)
             + jnp.dot(lo, ones_bd, preferred_element_type=F32))
        return [s[p * C:(p + 1) * C] for p in pairs]

    lane = lax.broadcasted_iota(jnp.int32, (1, PW), 1)
    in_h0 = lane < HEAD_DIM
    lane2 = lax.broadcasted_iota(jnp.int32, (1, 2 * PW), 1)
    in_h0_2 = (lane2 % PW) < HEAD_DIM

    r_p, k_p, v_p = pair_cols(r), pair_cols(k_new), pair_cols(v)
    kk_p = pair_cols(kk_all)
    asig_p = pair_cols(a_sig)
    ei_p, ee_p, ev_p, pc_p = pair_cols(e_incl), pair_cols(e_excl), pair_cols(e_inv), pair_cols(p_end)

    ss = head_sums([x * x for x in kk_p])
    kk_p = [kk_p[p] * lax.rsqrt(ss[p] + 1e-12) for p in pairs]
    a_t = [-kk_p[p] * ee_p[p] for p in pairs]
    r_t = [r_p[p] * ei_p[p] for p in pairs]
    b_hat = [kk_p[p] * asig_p[p] * ev_p[p] for p in pairs]
    k_hat = [k_p[p] * ev_p[p] for p in pairs]
    b_til = [b_hat[p] * pc_p[p] for p in pairs]
    k_til = [k_hat[p] * pc_p[p] for p in pairs]

    left = [jnp.concatenate([a_t[p], r_t[p]], axis=0) for p in pairs]
    n_pow, a_ak, a_rb, a_rk = {}, {}, {}, {}
    for (p, hh) in heads:
        lm = jnp.where(in_h0 if hh == 0 else ~in_h0, left[p], 0.0)
        ab = _mm_nt(lm, b_hat[p])
        ak = _mm_nt(lm, k_hat[p])
        n_pow[p, hh] = jnp.where(lower_strict, ab[:C], 0.0)
        a_ak[p, hh] = jnp.where(lower_strict, ak[:C], 0.0)
        a_rb[p, hh] = jnp.where(lower_incl, ab[C:], 0.0)
        a_rk[p, hh] = jnp.where(lower_incl, ak[C:], 0.0)

    def by_head(mats, x, mask):
        return jnp.where(mask, _mm(mats[0], x), _mm(mats[1], x))

    xs = [jnp.concatenate([a_t[p], by_head((a_ak[p, 0], a_ak[p, 1]), v_p[p], in_h0)], axis=1)
          for p in pairs]
    n_levels = C.bit_length() - 1
    for lvl in range(n_levels):
        xs = [xs[p] + by_head((n_pow[p, 0], n_pow[p, 1]), xs[p], in_h0_2) for p in pairs]
        if lvl + 1 < n_levels:
            n_pow = {hd: _mm(n_pow[hd], n_pow[hd]) for hd in heads}

    s_prev = [state_ref[p] for p in pairs]
    u_p = [_mm_nt(xs[p][:, :PW], s_prev[p]) + xs[p][:, PW:] for p in pairs]
    y_p = [_mm_nt(r_t[p], s_prev[p])
           + by_head((a_rb[p, 0], a_rb[p, 1]), u_p[p], in_h0)
           + by_head((a_rk[p, 0], a_rk[p, 1]), v_p[p], in_h0) for p in pairs]
    for p in pairs:
        upd = _mm_tn(jnp.concatenate([u_p[p], v_p[p]], axis=0),
                     jnp.concatenate([b_til[p], k_til[p]], axis=0))
        state_ref[p] = s_prev[p] * pc_p[p] + jnp.where(blockdiag, upd, 0.0)

    inv_n = 1.0 / HEAD_DIM
    mu_y = head_sums(y_p)
    yc = [y_p[p] - mu_y[p] * inv_n for p in pairs]
    var = head_sums([x * x for x in yc])
    rk_p = pair_cols(rk_ref[...])
    bon = head_sums([r_p[p] * k_p[p] * rk_p[p] for p in pairs])
    for p in pairs:
        cols = slice(p * PW, (p + 1) * PW)
        yn = yc[p] * lax.rsqrt(var[p] * inv_n + GN_EPS) * lg_ref[:, cols] + lb_ref[:, cols]
        o_ref[:, cols] = ((yn + bon[p] * v_p[p]) * gate[:, cols]).astype(o_ref.dtype)

    @pl.when(c == n_chunks - 1)
    def _():
        for (p, hh) in heads:
            lo = hh * HEAD_DIM
            s_out_ref[0, 2 * p + hh] = state_ref[p, lo:lo + HEAD_DIM, lo:lo + HEAD_DIM]
        shift_out_ref[0] = pr[C - 1:C, :]


def _rwkv(p_all, o_prev, shift_in, s0_bd, lw, *, row0, n_seq, seq_len, chunk):
    n_chunks = seq_len // chunk
    rb0 = row0 // chunk
    vec = lambda n: pl.BlockSpec((1, n), lambda b, c: (0, 0))
    mat = lambda k, n: pl.BlockSpec((k, n), lambda b, c: (0, 0))
    m = p_all.shape[0]
    kern = functools.partial(_rwkv_kernel, chunk=chunk, n_chunks=n_chunks)
    in_specs = [
        pl.BlockSpec((chunk, R_COLS), lambda b, c: (rb0 + b * n_chunks + c, COL_R // R_COLS)),
        pl.BlockSpec((1, 1, R_COLS), lambda b, c: (b, 0, 0)),
        pl.BlockSpec((1, N_PAIRS, LANES, LANES), lambda b, c: (b, 0, 0, 0)),
        vec(R_COLS), vec(W_H), mat(DECAY_LORA, W_H), vec(W_H), mat(AAA_LORA, W_H),
        mat(GATE_LORA, W_H), vec(W_H), vec(W_H), vec(W_H), vec(W_H), vec(W_H),
    ]
    args = [p_all, shift_in, s0_bd, lw['mu'], lw['w0'], lw['w2'], lw['a0'], lw['a2'], lw['g2'],
            lw['k_k'], lw['k_a'], lw['r_k'], lw['lnx_g'], lw['lnx_b']]
    aliases = {}
    if o_prev is not None:
        in_specs.append(pl.BlockSpec(memory_space=pl.ANY))
        args.append(o_prev)
        aliases = {len(args) - 1: 0}
        kern_fn = lambda *refs: kern(*refs[:14], *refs[15:])
    else:
        kern_fn = kern
    return pl.pallas_call(
        kern_fn,
        grid=(n_seq, n_chunks),
        in_specs=in_specs,
        out_specs=[pl.BlockSpec((chunk, W_H), lambda b, c: (rb0 + b * n_chunks + c, 0)),
                   pl.BlockSpec((1, N_HEADS, HEAD_DIM, HEAD_DIM), lambda b, c: (b, 0, 0, 0)),
                   pl.BlockSpec((1, 1, R_COLS), lambda b, c: (b, 0, 0))],
        out_shape=[jax.ShapeDtypeStruct((m, W_H), BF16),
                   jax.ShapeDtypeStruct((n_seq, N_HEADS, HEAD_DIM, HEAD_DIM), F32),
                   jax.ShapeDtypeStruct((n_seq, 1, R_COLS), F32)],
        scratch_shapes=[pltpu.VMEM((N_PAIRS, LANES, LANES), F32),
                        pltpu.VMEM((1, R_COLS), F32)],
        input_output_aliases=aliases,
        compiler_params=_cparams(("parallel", "arbitrary")),
        name=f"rwkv_c{chunk}",
    )(*args)


def _pair_block_diag(s):
    n = s.shape[0]
    s5 = s.reshape(n, N_PAIRS, 2, HEAD_DIM, HEAD_DIM)
    eye2 = jnp.eye(2, dtype=s.dtype)
    bd = s5[:, :, :, :, None, :] * eye2[None, None, :, None, :, None]
    return bd.reshape(n, N_PAIRS, LANES, LANES)


def _place_matrices():
    pk = np.zeros((W_H, N_HEADS * LANES), np.float32)
    pc = np.zeros((3 * LANES, N_HEADS * LANES), np.float32)
    ones = np.zeros((1, N_HEADS * LANES), np.float32)
    for h in range(N_HEADS):
        for j in range(HEAD_DIM):
            pk[h * HEAD_DIM + j, h * LANES + j] = 1.0
        for piece in range(3):
            pc[piece * LANES + h, h * LANES + CK_LANE0 + piece] = 1.0
            ones[0, h * LANES + CQ_LANE0 + piece] = 1.0
    return jnp.asarray(pk, BF16), jnp.asarray(pc, BF16), jnp.asarray(ones, F32)


def _fox_prep_kernel(k_ref, v_ref, fl_ref, bf_ref, pk_ref, pc_ref, ones_ref, *rest, has_prev):
    if has_prev:
        rest = rest[2:]
    kall_ref, vall_ref, logf_ref, c_ref, kaug_ref, vt_ref, carry_ref = rest

    @pl.when(pl.program_id(1) == 0)
    def _():
        carry_ref[...] = jnp.zeros_like(carry_ref)

    tc = fl_ref.shape[0]
    k = k_ref[...]
    v = v_ref[...]
    kall_ref[0] = k
    vall_ref[0] = v
    vt_ref[0] = v.T.astype(BF16)
    logf = _log_sigmoid(fl_ref[...] + bf_ref[...])
    logf_ref[...] = logf
    ti = lax.broadcasted_iota(jnp.int32, (tc, tc), 0)
    si = lax.broadcasted_iota(jnp.int32, (tc, tc), 1)
    c = _mm_exact_rhs((si <= ti).astype(BF16), logf) + carry_ref[...]
    carry_ref[...] = c[tc - 1:tc, :]
    c2 = c * LOG2E
    c_ref[...] = c2
    pieces = jnp.concatenate(_split3(-c2), axis=1)
    kaug = (jnp.dot(k.astype(BF16), pk_ref[...], preferred_element_type=F32)
            + jnp.dot(pieces, pc_ref[...], preferred_element_type=F32) + ones_ref[...])
    kaug_ref[...] = kaug.astype(BF16)


def _fox_prep(p_all, bf_pad, places, k_prev, v_prev, layer, depth, *, n_seq, seq_len):
    tc = _tile(seq_len, (PREP_BLOCK, 256, 128))
    nb = seq_len // tc
    rows = n_seq * seq_len
    has_prev = k_prev is not None
    const = lambda a: pl.BlockSpec(a.shape, lambda b, j: (0, 0))
    in_specs = [pl.BlockSpec((tc, W_H), lambda b, j: (b * nb + j, COL_K // W_H)),
                pl.BlockSpec((tc, W_H), lambda b, j: (b * nb + j, COL_V // W_H)),
                pl.BlockSpec((tc, LANES), lambda b, j: (b * nb + j, COL_FL // LANES)),
                pl.BlockSpec((1, LANES), lambda b, j: (0, 0)),
                const(places[0]), const(places[1]), const(places[2])]
    args = [p_all, p_all, p_all, bf_pad, *places]
    aliases = {}
    if has_prev:
        in_specs += [pl.BlockSpec(memory_space=pl.ANY), pl.BlockSpec(memory_space=pl.ANY)]
        args += [k_prev, v_prev]
        aliases = {7: 0, 8: 1}
    rowblk = lambda n: pl.BlockSpec((tc, n), lambda b, j: (b * nb + j, 0))
    stacked = pl.BlockSpec((1, tc, W_H), lambda b, j: (layer, b * nb + j, 0))
    return pl.pallas_call(
        functools.partial(_fox_prep_kernel, has_prev=has_prev),
        grid=(n_seq, nb),
        in_specs=in_specs,
        out_specs=[stacked, stacked, rowblk(LANES), rowblk(LANES), rowblk(N_HEADS * LANES),
                   pl.BlockSpec((1, W_H, tc), lambda b, j: (b, 0, j))],
        out_shape=[jax.ShapeDtypeStruct((depth, rows, W_H), F32),
                   jax.ShapeDtypeStruct((depth, rows, W_H), F32),
                   jax.ShapeDtypeStruct((rows, LANES), F32),
                   jax.ShapeDtypeStruct((rows, LANES), F32),
                   jax.ShapeDtypeStruct((rows, N_HEADS * LANES), BF16),
                   jax.ShapeDtypeStruct((n_seq, W_H, seq_len), BF16)],
        scratch_shapes=[pltpu.VMEM((1, LANES), F32)],
        input_output_aliases=aliases,
        compiler_params=_cparams(("parallel", "arbitrary")),
        name="fox_prep",
    )(*args)


def _fox_prompt_kernel(q_ref, c_ref, kaug_ref, vt_ref, o_ref, m_ref, l_ref, acc_ref, *, blk):
    hp = pl.program_id(1)
    i = pl.program_id(2)
    lane = lax.broadcasted_iota(jnp.int32, (1, LANES), 1)
    c_blk = c_ref[...]
    q_all = q_ref[...] * (SCALE * LOG2E)
    q_aug = []
    for hh in range(2):
        q_h = q_all if hh == 0 else pltpu.roll(q_all, HEAD_DIM, 1)
        cq = jnp.sum(jnp.where(lane == 2 * hp + hh, c_blk, 0.0), axis=-1, keepdims=True)
        hi, mid, lo = _split3(cq)
        aug = jnp.where(lane < HEAD_DIM, q_h,
                        jnp.where(lane == CQ_LANE0, hi.astype(F32),
                                  jnp.where(lane == CQ_LANE0 + 1, mid.astype(F32),
                                            jnp.where(lane == CQ_LANE0 + 2, lo.astype(F32),
                                                      jnp.where(lane < CK_LANE0 + 3, 1.0, 0.0)))))
        q_aug.append(aug.astype(BF16))

    m_ref[...] = jnp.full(m_ref.shape, -jnp.inf, F32)
    l_ref[...] = jnp.zeros(l_ref.shape, F32)
    acc_ref[...] = jnp.zeros(acc_ref.shape, F32)
    key_i = lax.broadcasted_iota(jnp.int32, (blk, blk), 0)
    qry_i = lax.broadcasted_iota(jnp.int32, (blk, blk), 1)
    causal = key_i <= qry_i

    def step(j, masked):
        start = pl.multiple_of(j * blk, blk)
        vt = vt_ref[0, :, pl.ds(start, blk)]
        sts = [lax.dot_general(kaug_ref[pl.ds(start, blk), hh * LANES:(hh + 1) * LANES], q_aug[hh],
                               (((1,), (1,)), ((), ())), preferred_element_type=F32)
               for hh in range(2)]
        for hh in range(2):
            st = sts[hh]
            if masked:
                st = jnp.where(causal, st, -jnp.inf)
            m_old = m_ref[hh]
            m_new = jnp.maximum(m_old, jnp.max(st, axis=0, keepdims=True))
            alpha = jnp.exp2(m_old - m_new)
            pt = jnp.exp2(st - m_new)
            l_ref[hh] = alpha * l_ref[hh] + jnp.sum(pt, axis=0, keepdims=True)
            m_ref[hh] = m_new
            acc_ref[hh] = alpha * acc_ref[hh] + jnp.dot(vt, pt.astype(BF16),
                                                        preferred_element_type=F32)

    def body(j, carry):
        step(j, False)
        return carry

    lax.fori_loop(0, i, body, 0)
    step(i, True)
    chan = lax.broadcasted_iota(jnp.int32, (LANES, 1), 0)
    out_t = jnp.where(chan < HEAD_DIM, acc_ref[0] / l_ref[0], acc_ref[1] / l_ref[1])
    o_ref[...] = out_t.T.astype(o_ref.dtype)


def _fox_prompt(p_all, c, kaug, v_t, *, n_seq, seq_len):
    blk = _tile(seq_len, (ATT_BLOCK, 256, 128))
    nq = seq_len // blk
    m = p_all.shape[0]
    kern = functools.partial(_fox_prompt_kernel, blk=blk)
    return pl.pallas_call(
        kern,
        grid=(n_seq, N_PAIRS, nq),
        in_specs=[pl.BlockSpec((blk, LANES), lambda b, hp, i: (b * nq + i, COL_Q // LANES + hp)),
                  pl.BlockSpec((blk, LANES), lambda b, hp, i: (b * nq + i, 0)),
                  pl.BlockSpec((seq_len, KAUG_W), lambda b, hp, i: (b, hp)),
                  pl.BlockSpec((1, LANES, seq_len), lambda b, hp, i: (b, hp, 0))],
        out_specs=pl.BlockSpec((blk, LANES), lambda b, hp, i: (b * nq + i, hp)),
        out_shape=jax.ShapeDtypeStruct((m, W_H), BF16),
        scratch_shapes=[pltpu.VMEM((2, 1, blk), F32), pltpu.VMEM((2, 1, blk), F32),
                        pltpu.VMEM((2, LANES, blk), F32)],
        compiler_params=_cparams(("parallel", "parallel", "arbitrary")),
        name="fox_prompt",
    )(p_all, c, kaug, v_t)


def _fox_sample_kernel(q_ref, k_ref, v_ref, fl_ref, bf_ref, ck_ref, cv_ref, clt_ref, *rest,
                       t_new, past, n_alias):
    rest = rest[n_alias:]
    o_ref, logf_ref, kall_ref, vall_ref = rest
    T = t_new
    kall_ref[0] = k_ref[...]
    vall_ref[0] = v_ref[...]
    logf = _log_sigmoid(fl_ref[...] + bf_ref[...])
    logf_ref[...] = logf
    logf_pad = jnp.concatenate([logf, jnp.zeros((LANES - T, LANES), F32)], axis=0)
    ti = lax.broadcasted_iota(jnp.int32, (LANES, LANES), 0)
    si = lax.broadcasted_iota(jnp.int32, (LANES, LANES), 1)
    cnew_pad = _mm_exact_rhs((si <= ti).astype(BF16), logf_pad)
    cnew_t = cnew_pad.T
    cnew = cnew_pad[0:T, :]
    suf = clt_ref[0, 0]
    pos = lax.broadcasted_iota(jnp.int32, suf.shape, 1)
    total = suf
    d = 1
    while d < past:
        total = total + jnp.where(pos + d < past, pltpu.roll(total, past - d, 1), 0.0)
        d *= 2
    suf = total - suf

    lane = lax.broadcasted_iota(jnp.int32, (T, LANES), 1)
    rowi = lax.broadcasted_iota(jnp.int32, (T, T), 0)
    coli = lax.broadcasted_iota(jnp.int32, (T, T), 1)
    for h in range(N_HEADS):
        sl = slice(h * HEAD_DIM, (h + 1) * HEAD_DIM)
        q = q_ref[:, sl] * SCALE
        cq = jnp.sum(jnp.where(lane == h, cnew, 0.0), axis=-1, keepdims=True)
        s_c = _mm_nt(q, ck_ref[0, 0, :, sl]) + (cq + suf[h:h + 1, :])
        s_n = _mm_nt(q, k_ref[:, sl]) + (cq - cnew_t[h:h + 1, 0:T])
        s_n = jnp.where(coli <= rowi, s_n, -jnp.inf)
        m = jnp.maximum(jnp.max(s_c, axis=-1, keepdims=True), jnp.max(s_n, axis=-1, keepdims=True))
        p_c = jnp.exp(s_c - m)
        p_n = jnp.exp(s_n - m)
        l = jnp.sum(p_c, axis=-1, keepdims=True) + jnp.sum(p_n, axis=-1, keepdims=True)
        o = _mm(p_c, cv_ref[0, 0, :, sl]) + _mm(p_n, v_ref[:, sl])
        o_ref[:, sl] = (o / l).astype(o_ref.dtype)


def _fox_sample(p_all, bf_pad, cache_k, cache_v, cache_logf_t, o_prev, k_prev, v_prev, layer, depth,
                *, row0, n_seq, t_new):
    past = cache_k.shape[2]
    rb0 = row0 // t_new
    blk512 = lambda col: pl.BlockSpec((t_new, W_H), lambda b: (rb0 + b, col // W_H))
    in_specs = [blk512(COL_Q), blk512(COL_K), blk512(COL_V),
                pl.BlockSpec((t_new, LANES), lambda b: (rb0 + b, COL_FL // LANES)),
                pl.BlockSpec((1, LANES), lambda b: (0, 0)),
                pl.BlockSpec((1, 1, past, W_H), lambda b: (layer, b, 0, 0)),
                pl.BlockSpec((1, 1, past, W_H), lambda b: (layer, b, 0, 0)),
                pl.BlockSpec((1, 1, N_HEADS, past), lambda b: (layer, b, 0, 0)),
                pl.BlockSpec(memory_space=pl.ANY)]
    args = [p_all, p_all, p_all, p_all, bf_pad, cache_k, cache_v, cache_logf_t, o_prev]
    aliases = {8: 0}
    if k_prev is not None:
        in_specs += [pl.BlockSpec(memory_space=pl.ANY), pl.BlockSpec(memory_space=pl.ANY)]
        args += [k_prev, v_prev]
        aliases.update({9: 2, 10: 3})
    kern = functools.partial(_fox_sample_kernel, t_new=t_new, past=past, n_alias=len(aliases))
    stacked = pl.BlockSpec((1, t_new, W_H), lambda b: (layer, b, 0))
    return pl.pallas_call(
        kern,
        grid=(n_seq,),
        in_specs=in_specs,
        out_specs=[pl.BlockSpec((t_new, W_H), lambda b: (rb0 + b, 0)),
                   pl.BlockSpec((t_new, LANES), lambda b: (b, 0)), stacked, stacked],
        out_shape=[jax.ShapeDtypeStruct(o_prev.shape, BF16),
                   jax.ShapeDtypeStruct((n_seq * t_new, LANES), F32),
                   jax.ShapeDtypeStruct((depth, n_seq * t_new, W_H), F32),
                   jax.ShapeDtypeStruct((depth, n_seq * t_new, W_H), F32)],
        input_output_aliases=aliases,
        compiler_params=_cparams(("parallel",)),
        name="fox_sample",
    )(*args)


def _merge_kernel(x_ref, oa_ref, ob_ref, ga_ref, gb_ref, pa_ref, pb_ref, wo_ref, o_ref):
    ma = jnp.dot(oa_ref[...], pa_ref[...], preferred_element_type=F32)
    mb = jnp.dot(ob_ref[...], pb_ref[...], preferred_element_type=F32)
    mix = _sigmoid(ga_ref[...]) * ma + _sigmoid(gb_ref[...]) * mb
    o_ref[...] = x_ref[...] + jnp.dot(mix.astype(BF16), wo_ref[...], preferred_element_type=F32)


def _merge(x, o_a, o_b, p_all, pa, pb, wo, tm):
    m, d = x.shape
    row = lambda n, col=0: pl.BlockSpec((tm, n), lambda i: (i, col))
    full = lambda a: pl.BlockSpec(a.shape, lambda i: (0, 0))
    return pl.pallas_call(
        _merge_kernel,
        grid=(m // tm,),
        in_specs=[row(d), row(W_H), row(W_H), row(d, COL_GA // d), row(d, COL_GB // d),
                  full(pa), full(pb), full(wo)],
        out_specs=row(d),
        out_shape=jax.ShapeDtypeStruct((m, d), F32),
        compiler_params=_cparams(("parallel",)),
        name="merge",
    )(x, o_a, o_b, p_all, p_all, pa, pb, wo)


def _ffn_kernel(x_ref, g_ref, wg_ref, wu_ref, wd_ref, fg_ref, o_ref, h_ref, acc_ref, *, final):
    f = pl.program_id(1)

    @pl.when(f == 0)
    def _():
        x = x_ref[...]
        y = x * lax.rsqrt(jnp.mean(x * x, axis=-1, keepdims=True) + EPS) * g_ref[...]
        h_ref[...] = y.astype(BF16)
        acc_ref[...] = x

    h = h_ref[...]
    gate = jnp.dot(h, wg_ref[...], preferred_element_type=F32)
    up = jnp.dot(h, wu_ref[...], preferred_element_type=F32)
    act = gate * _sigmoid(gate) * up
    acc_ref[...] += jnp.dot(act.astype(BF16), wd_ref[...], preferred_element_type=F32)

    @pl.when(f == pl.num_programs(1) - 1)
    def _():
        y = acc_ref[...]
        if final:
            y = y * lax.rsqrt(jnp.mean(y * y, axis=-1, keepdims=True) + EPS) * fg_ref[...]
        o_ref[...] = y


def _ffn(x, g, wg, wu, wd, fg, tm, tf, final):
    m, d = x.shape
    dff = wg.shape[1]
    return pl.pallas_call(
        functools.partial(_ffn_kernel, final=final),
        grid=(m // tm, dff // tf),
        in_specs=[pl.BlockSpec((tm, d), lambda i, f: (i, 0)),
                  pl.BlockSpec((1, d), lambda i, f: (0, 0)),
                  pl.BlockSpec((d, tf), lambda i, f: (0, f)),
                  pl.BlockSpec((d, tf), lambda i, f: (0, f)),
                  pl.BlockSpec((tf, d), lambda i, f: (f, 0)),
                  pl.BlockSpec((1, d), lambda i, f: (0, 0))],
        out_specs=pl.BlockSpec((tm, d), lambda i, f: (i, 0)),
        out_shape=jax.ShapeDtypeStruct((m, d), F32),
        scratch_shapes=[pltpu.VMEM((tm, d), BF16), pltpu.VMEM((tm, d), F32)],
        compiler_params=_cparams(("parallel", "arbitrary")),
        name="ffn",
    )(x, g, wg, wu, wd, fg)


def kernel(x_prompt, x_sample, cache_fox_k, cache_fox_v, cache_fox_logf, state_rwkv, state_shift,
           norm1_g, w_in, rwkv_mu, rwkv_w0, rwkv_w2, rwkv_a0, rwkv_a2, rwkv_g2, rwkv_k_k, rwkv_k_a,
           rwkv_r_k, rwkv_lnx_g, rwkv_lnx_b, fox_bf, p_a, p_b, w_out, norm2_g, w_gate, w_up, w_down,
           final_g):
    n_b, seq, d = x_prompt.shape
    n_db, t_new, _ = x_sample.shape
    depth = w_in.shape[0]
    past = cache_fox_k.shape[2]
    d_ff = w_gate.shape[2]
    m_prompt = n_b * seq
    m_all = m_prompt + n_db * t_new

    o_q = R_COLS
    o_fl = R_COLS + 3 * W_H
    o_g = o_fl + N_HEADS
    w_cat = jnp.concatenate(
        [w_in[:, :, o_g:o_g + 2 * d], w_in[:, :, o_q:o_q + 3 * W_H], w_in[:, :, 0:R_COLS],
         w_in[:, :, o_fl:o_fl + N_HEADS],
         jnp.zeros((depth, d, P_COLS - COL_FL - N_HEADS), w_in.dtype)], axis=-1).astype(BF16)
    bf_pad = jnp.pad(fox_bf, ((0, 0), (0, LANES - N_HEADS)))
    pa_b, pb_b, wo_b = p_a.astype(BF16), p_b.astype(BF16), w_out.astype(BF16)
    wg_b, wu_b, wd_b = w_gate.astype(BF16), w_up.astype(BF16), w_down.astype(BF16)
    w2_b, a2_b, g2_b = rwkv_w2.astype(BF16), rwkv_a2.astype(BF16), rwkv_g2.astype(BF16)
    cache_logf_t = jnp.swapaxes(cache_fox_logf, 2, 3)
    cache_k = cache_fox_k.reshape(depth, n_db, past, W_H)
    cache_v = cache_fox_v.reshape(depth, n_db, past, W_H)
    places = _place_matrices()

    tm = _tile(m_all)
    tm_w = _tile(m_all, (1536, 512, 256, 128, 64, 32, 16, 8))
    tn = _tile(P_COLS, (512,))
    tf = _tile(d_ff, (256, 128))
    x = jnp.concatenate([x_prompt.reshape(m_prompt, d), x_sample.reshape(n_db * t_new, d)], axis=0)
    zero_shift = jnp.zeros((n_b, 1, R_COLS), F32)
    zero_state = jnp.zeros((n_b, N_PAIRS, LANES, LANES), F32)

    outs = {k: [] for k in ('lfp', 'sp', 'shp', 'lfd', 'sd', 'shd')}
    kp = vp = kd = vd = None
    for l in range(depth):
        row = lambda a: a[l].reshape(1, -1)
        lw = dict(mu=row(rwkv_mu), w0=row(rwkv_w0), w2=w2_b[l], a0=row(rwkv_a0), a2=a2_b[l],
                  g2=g2_b[l], k_k=row(rwkv_k_k), k_a=row(rwkv_k_a), r_k=row(rwkv_r_k),
                  lnx_g=row(rwkv_lnx_g), lnx_b=row(rwkv_lnx_b))
        p_all = _norm_matmul(x, row(norm1_g), w_cat[l], tm_w, tn)

        o_a, s_p, sh_p = _rwkv(p_all, None, zero_shift, zero_state, lw, row0=0, n_seq=n_b,
                               seq_len=seq, chunk=PROMPT_CHUNK)
        o_a, s_d, sh_d = _rwkv(p_all, o_a, state_shift[l], _pair_block_diag(state_rwkv[l]), lw,
                               row0=m_prompt, n_seq=n_db, seq_len=t_new, chunk=t_new)

        kp, vp, logf_p, c_p, kaug, v_t = _fox_prep(p_all, bf_pad[l:l + 1], places, kp, vp, l, depth,
                                                    n_seq=n_b, seq_len=seq)
        o_b = _fox_prompt(p_all, c_p, kaug, v_t, n_seq=n_b, seq_len=seq)
        o_b, logf_d, kd, vd = _fox_sample(p_all, bf_pad[l:l + 1], cache_k, cache_v, cache_logf_t,
                                          o_b, kd, vd, l, depth, row0=m_prompt, n_seq=n_db,
                                          t_new=t_new)

        x = _merge(x, o_a, o_b, p_all, pa_b[l], pb_b[l], wo_b[l], tm)
        x = _ffn(x, row(norm2_g), wg_b[l], wu_b[l], wd_b[l], final_g.reshape(1, -1), tm_w, tf,
                 final=(l == depth - 1))

        outs['lfp'].append(logf_p[:, :N_HEADS].reshape(n_b, seq, N_HEADS))
        outs['lfd'].append(logf_d[:, :N_HEADS].reshape(n_db, t_new, N_HEADS))
        outs['sp'].append(s_p)
        outs['shp'].append(sh_p)
        outs['sd'].append(s_d)
        outs['shd'].append(sh_d)

    st = {k: jnp.stack(v) for k, v in outs.items()}
    heads = lambda a, nb, t: a.reshape(depth, nb, t, N_HEADS, HEAD_DIM)
    y_prompt = x[:m_prompt].reshape(n_b, seq, d)
    y_sample = x[m_prompt:].reshape(n_db, t_new, d)
    return (y_prompt, y_sample, heads(kp, n_b, seq), heads(vp, n_b, seq), st['lfp'], st['sp'],
            st['shp'], heads(kd, n_db, t_new), heads(vd, n_db, t_new), st['lfd'], st['sd'],
            st['shd'])
```

```python
import functools
import math

import numpy as np
import jax
import jax.numpy as jnp
from jax import lax
from jax.experimental import pallas as pl
from jax.experimental.pallas import tpu as pltpu

F32 = jnp.float32
BF16 = jnp.bfloat16

HEAD_DIM = 64
N_HEADS = 8
N_PAIRS = N_HEADS // 2
W_H = N_HEADS * HEAD_DIM
DECAY_LORA = 64
AAA_LORA = 64
GATE_LORA = 128
R_COLS = 3 * W_H + DECAY_LORA + AAA_LORA + GATE_LORA
EPS = 1e-6
GN_EPS = 64e-5
SCALE = HEAD_DIM ** -0.5
LOG2E = 1.4426950408889634

LANES = 128
VMEM_LIMIT_BYTES = 56 * 1024 * 1024

COL_GA = 0
COL_GB = 1024
COL_Q = 2048
COL_K = 2560
COL_V = 3072
COL_R = 3584
COL_FL = 5376
P_COLS = 5632
PROMPT_CHUNK = 64
PROMPT_SUBCHUNKS = 4
ATT_BLOCK = 512
PREP_BLOCK = 512
KAUG_W = 2 * LANES
CQ_LANE0 = HEAD_DIM
CK_LANE0 = HEAD_DIM + 3


def _cparams(sem):
    return pltpu.CompilerParams(dimension_semantics=sem, vmem_limit_bytes=VMEM_LIMIT_BYTES)


def _mm(a, b):
    return jnp.dot(a.astype(BF16), b.astype(BF16), preferred_element_type=F32)


def _mm_nt(a, b):
    return lax.dot_general(a.astype(BF16), b.astype(BF16), (((1,), (1,)), ((), ())),
                           preferred_element_type=F32)


def _mm_tn(a, b):
    return lax.dot_general(a.astype(BF16), b.astype(BF16), (((0,), (0,)), ((), ())),
                           preferred_element_type=F32)


def _split3(x):
    hi = x.astype(BF16)
    r = x - hi.astype(F32)
    mid = r.astype(BF16)
    lo = (r - mid.astype(F32)).astype(BF16)
    return hi, mid, lo


def _mm_exact_rhs(m01, x):
    hi, mid, lo = _split3(x)
    d = lambda p: jnp.dot(m01, p, preferred_element_type=F32)
    return d(hi) + d(mid) + d(lo)


def _log_sigmoid(z):
    return jnp.minimum(z, 0.0) - jnp.log1p(jnp.exp(-jnp.abs(z)))


def _sigmoid(z):
    return 1.0 / (1.0 + jnp.exp(-z))


def _tile(m, cands=(512, 256, 128, 64, 32, 16, 8)):
    for t in cands:
        if m % t == 0:
            return t
    raise ValueError(m)


def _norm_matmul_kernel(x_ref, g_ref, w_ref, o_ref, h_ref):
    @pl.when(pl.program_id(1) == 0)
    def _():
        x = x_ref[...]
        y = x * lax.rsqrt(jnp.mean(x * x, axis=-1, keepdims=True) + EPS) * g_ref[...]
        h_ref[...] = y.astype(BF16)

    o_ref[...] = lax.dot_general(h_ref[...], w_ref[...], (((1,), (1,)), ((), ())),
                                 preferred_element_type=F32)


def _norm_matmul(x, g, w_t, tm, tn):
    m, d = x.shape
    n = w_t.shape[0]
    return pl.pallas_call(
        _norm_matmul_kernel,
        grid=(m // tm, n // tn),
        in_specs=[pl.BlockSpec((tm, d), lambda i, j: (i, 0)),
                  pl.BlockSpec((1, d), lambda i, j: (0, 0)),
                  pl.BlockSpec((tn, d), lambda i, j: (j, 0))],
        out_specs=pl.BlockSpec((tm, tn), lambda i, j: (i, j)),
        out_shape=jax.ShapeDtypeStruct((m, n), F32),
        scratch_shapes=[pltpu.VMEM((tm, d), BF16)],
        compiler_params=_cparams(("parallel", "arbitrary")),
        name="in_proj",
    )(x, g, w_t)


def _rwkv_kernel(pr_ref, shift_ref, s0_ref, mu_ref, w0_ref, w2_ref, a0_ref, a2_ref, g2_ref,
                 kk_ref, ka_ref, rk_ref, lg_ref, lb_ref,
                 o_ref, s_out_ref, shift_out_ref, state_ref, prev_ref, *, chunk, n_sub, n_steps):
    c = pl.program_id(1)
    C = chunk
    R = n_sub * C
    PW = LANES
    subs = range(n_sub)
    pairs = range(N_PAIRS)
    units = [(s, p) for s in subs for p in pairs]
    uheads = [(s, p, hh) for (s, p) in units for hh in range(2)]

    @pl.when(c == 0)
    def _():
        state_ref[...] = jnp.zeros(state_ref.shape, F32)
        for h in range(N_HEADS):
            lo = (h % 2) * HEAD_DIM
            state_ref[h // 2, lo:lo + HEAD_DIM, lo:lo + HEAD_DIM] = s0_ref[0, h]
        prev_ref[...] = shift_ref[0]

    pr = pr_ref[...]
    row = lax.broadcasted_iota(jnp.int32, (R, 1), 0)
    shifted = jnp.where(row == 0, prev_ref[...], pltpu.roll(pr, 1, 0))
    u = pr + (shifted - pr) * mu_ref[...]
    prev_ref[...] = pr[R - 1:R, :]

    r = u[:, 0:W_H]
    k = u[:, W_H:2 * W_H]
    v = u[:, 2 * W_H:3 * W_H]
    wl = u[:, 3 * W_H:3 * W_H + DECAY_LORA]
    al = u[:, 3 * W_H + DECAY_LORA:3 * W_H + DECAY_LORA + AAA_LORA]
    gl = u[:, 3 * W_H + DECAY_LORA + AAA_LORA:R_COLS]

    w_raw = w0_ref[...] + _mm(jnp.tanh(wl), w2_ref[...])
    logw = -jnp.exp(_log_sigmoid(w_raw) - 0.5)
    a_sig = _sigmoid(a0_ref[...] + _mm(al, a2_ref[...]))
    gate = _mm(_sigmoid(gl), g2_ref[...])
    kk_all = k * kk_ref[...]
    k_new = k * (1.0 + (a_sig - 1.0) * ka_ref[...])

    ti = lax.broadcasted_iota(jnp.int32, (R, R), 0)
    si = lax.broadcasted_iota(jnp.int32, (R, R), 1)
    cum_mask = (si <= ti) & ((si // C) == (ti // C))
    g_cum = _mm_exact_rhs(cum_mask.astype(BF16), logw)
    e_incl = jnp.exp(g_cum)
    e_excl = jnp.exp(g_cum - logw)
    e_inv = jnp.exp(-g_cum)

    tci = lax.broadcasted_iota(jnp.int32, (C, C), 0)
    sci = lax.broadcasted_iota(jnp.int32, (C, C), 1)
    lower_incl = sci <= tci
    lower_strict = sci < tci

    bi = lax.broadcasted_iota(jnp.int32, (PW, PW), 0) // HEAD_DIM
    bj = lax.broadcasted_iota(jnp.int32, (PW, PW), 1) // HEAD_DIM
    blockdiag = bi == bj
    ones_bd = blockdiag.astype(BF16)

    def unit_cols(x):
        return {(s, p): x[s * C:(s + 1) * C, p * PW:(p + 1) * PW] for (s, p) in units}

    def head_sums(xs):
        x = jnp.concatenate([xs[un] for un in units], axis=0)
        s = jnp.dot(x.astype(BF16), ones_bd, preferred_element_type=F32)
        return {un: s[i * C:(i + 1) * C] for i, un in enumerate(units)}

    lane = lax.broadcasted_iota(jnp.int32, (1, PW), 1)
    in_h0 = lane < HEAD_DIM
    lane2 = lax.broadcasted_iota(jnp.int32, (1, 2 * PW), 1)
    in_h0_2 = (lane2 % PW) < HEAD_DIM

    r_u, k_u, v_u = unit_cols(r), unit_cols(k_new), unit_cols(v)
    kk_u = unit_cols(kk_all)
    asig_u = unit_cols(a_sig)
    ei_u, ee_u, ev_u = unit_cols(e_incl), unit_cols(e_excl), unit_cols(e_inv)
    pc_u = {(s, p): e_incl[(s + 1) * C - 1:(s + 1) * C, p * PW:(p + 1) * PW] for (s, p) in units}

    ss = head_sums({un: kk_u[un] * kk_u[un] for un in units})
    kk_u = {un: kk_u[un] * lax.rsqrt(ss[un] + 1e-12) for un in units}
    a_t = {un: -kk_u[un] * ee_u[un] for un in units}
    r_t = {un: r_u[un] * ei_u[un] for un in units}
    b_hat = {un: kk_u[un] * asig_u[un] * ev_u[un] for un in units}
    k_hat = {un: k_u[un] * ev_u[un] for un in units}
    b_til = {un: b_hat[un] * pc_u[un] for un in units}
    k_til = {un: k_hat[un] * pc_u[un] for un in units}

    left = {un: jnp.concatenate([a_t[un], r_t[un]], axis=0) for un in units}
    right = {un: jnp.concatenate([b_hat[un], k_hat[un]], axis=0) for un in units}
    tci2 = lax.broadcasted_iota(jnp.int32, (C, 2 * C), 0)
    sci2 = lax.broadcasted_iota(jnp.int32, (C, 2 * C), 1) % C
    strict2 = sci2 < tci2
    incl2 = sci2 <= tci2
    n_pow, a_rb, top, bot = {}, {}, {}, {}
    for (s, p, hh) in uheads:
        lm = jnp.where(in_h0 if hh == 0 else ~in_h0, left[s, p], 0.0)
        quad = _mm_nt(lm, right[s, p])
        top[s, p, hh] = jnp.where(strict2, quad[:C], 0.0)
        bot[s, p, hh] = jnp.where(incl2, quad[C:], 0.0)
        n_pow[s, p, hh] = top[s, p, hh][:, :C]
        a_rb[s, p, hh] = bot[s, p, hh][:, :C]

    def by_head(mats, un, x, mask):
        return jnp.where(mask, _mm(mats[un + (0,)], x), _mm(mats[un + (1,)], x))

    zv = {un: jnp.concatenate([jnp.zeros((C, PW), F32), v_u[un]], axis=0) for un in units}
    xs = {un: jnp.concatenate([a_t[un], by_head(top, un, zv[un], in_h0)], axis=1)
          for un in units}
    y_v = {un: by_head(bot, un, zv[un], in_h0) for un in units}
    n_levels = C.bit_length() - 1
    for lvl in range(n_levels):
        xs = {un: xs[un] + by_head(n_pow, un, xs[un], in_h0_2) for un in units}
        if lvl + 1 < n_levels:
            n_pow = {hd: _mm(n_pow[hd], n_pow[hd]) for hd in uheads}

    st = [state_ref[p] for p in pairs]
    y_u = {}
    for s in subs:
        u_p = [_mm_nt(xs[s, p][:, :PW], st[p]) + xs[s, p][:, PW:] for p in pairs]
        for p in pairs:
            y_u[s, p] = (_mm_nt(r_t[s, p], st[p]) + by_head(a_rb, (s, p), u_p[p], in_h0)
                         + y_v[s, p])
        upd = [_mm_tn(jnp.concatenate([u_p[p], v_u[s, p]], axis=0),
                      jnp.concatenate([b_til[s, p], k_til[s, p]], axis=0)) for p in pairs]
        st = [st[p] * pc_u[s, p] + jnp.where(blockdiag, upd[p], 0.0) for p in pairs]
    for p in pairs:
        state_ref[p] = st[p]

    inv_n = 1.0 / HEAD_DIM
    mu_y = head_sums(y_u)
    yc = {un: y_u[un] - mu_y[un] * inv_n for un in units}
    var = head_sums({un: yc[un] * yc[un] for un in units})
    rk_row = rk_ref[...]
    bon = head_sums({(s, p): r_u[s, p] * k_u[s, p] * rk_row[:, p * PW:(p + 1) * PW]
                     for (s, p) in units})
    for (s, p) in units:
        cols = slice(p * PW, (p + 1) * PW)
        rows = slice(s * C, (s + 1) * C)
        yn = yc[s, p] * lax.rsqrt(var[s, p] * inv_n + GN_EPS) * lg_ref[:, cols] + lb_ref[:, cols]
        o_ref[rows, cols] = ((yn + bon[s, p] * v_u[s, p]) * gate[rows, cols]).astype(o_ref.dtype)

    @pl.when(c == n_steps - 1)
    def _():
        for h in range(N_HEADS):
            lo = (h % 2) * HEAD_DIM
            s_out_ref[0, h] = state_ref[h // 2, lo:lo + HEAD_DIM, lo:lo + HEAD_DIM]
        shift_out_ref[0] = pr[R - 1:R, :]


def _rwkv(p_all, shift_in, s0, lw, *, row0, n_seq, seq_len, chunk, n_sub):
    rows = chunk * n_sub
    n_steps = seq_len // rows
    rb0 = row0 // rows
    vec = lambda n: pl.BlockSpec((1, n), lambda b, c: (0, 0))
    mat = lambda k, n: pl.BlockSpec((k, n), lambda b, c: (0, 0))
    kern = functools.partial(_rwkv_kernel, chunk=chunk, n_sub=n_sub, n_steps=n_steps)
    state_spec = pl.BlockSpec((1, N_HEADS, HEAD_DIM, HEAD_DIM), lambda b, c: (b, 0, 0, 0))
    shift_spec = pl.BlockSpec((1, 1, R_COLS), lambda b, c: (b, 0, 0))
    return pl.pallas_call(
        kern,
        grid=(n_seq, n_steps),
        in_specs=[
            pl.BlockSpec((rows, R_COLS), lambda b, c: (rb0 + b * n_steps + c, COL_R // R_COLS)),
            shift_spec, state_spec,
            vec(R_COLS), vec(W_H), mat(DECAY_LORA, W_H), vec(W_H), mat(AAA_LORA, W_H),
            mat(GATE_LORA, W_H), vec(W_H), vec(W_H), vec(W_H), vec(W_H), vec(W_H)],
        out_specs=[pl.BlockSpec((rows, W_H), lambda b, c: (b * n_steps + c, 0)),
                   state_spec, shift_spec],
        out_shape=[jax.ShapeDtypeStruct((n_seq * seq_len, W_H), BF16),
                   jax.ShapeDtypeStruct((n_seq, N_HEADS, HEAD_DIM, HEAD_DIM), F32),
                   jax.ShapeDtypeStruct((n_seq, 1, R_COLS), F32)],
        scratch_shapes=[pltpu.VMEM((N_PAIRS, LANES, LANES), F32),
                        pltpu.VMEM((1, R_COLS), F32)],
        compiler_params=_cparams(("parallel", "arbitrary")),
        name=f"rwkv_c{chunk}",
    )(p_all, shift_in, s0, lw['mu'], lw['w0'], lw['w2'], lw['a0'], lw['a2'], lw['g2'],
      lw['k_k'], lw['k_a'], lw['r_k'], lw['lnx_g'], lw['lnx_b'])


def _place_matrices():
    pk = np.zeros((W_H, N_HEADS * LANES), np.float32)
    pc = np.zeros((3 * LANES, N_HEADS * LANES), np.float32)
    ones = np.zeros((1, N_HEADS * LANES), np.float32)
    for h in range(N_HEADS):
        for j in range(HEAD_DIM):
            pk[h * HEAD_DIM + j, h * LANES + j] = 1.0
        for piece in range(3):
            pc[piece * LANES + h, h * LANES + CK_LANE0 + piece] = 1.0
            ones[0, h * LANES + CQ_LANE0 + piece] = 1.0
    return jnp.asarray(pk, BF16), jnp.asarray(pc, BF16), jnp.asarray(ones, F32)


def _fox_prep_kernel(k_ref, v_ref, fl_ref, bf_ref, pk_ref, pc_ref, ones_ref, kprev_ref, vprev_ref,
                     kall_ref, vall_ref, logft_ref, c_ref, kaug_ref, vt_ref, carry_ref):
    del kprev_ref, vprev_ref

    @pl.when(pl.program_id(1) == 0)
    def _():
        carry_ref[...] = jnp.zeros_like(carry_ref)

    tc = fl_ref.shape[0]
    k = k_ref[...]
    v_t = v_ref[...].T
    kall_ref[0, 0] = k.T
    vall_ref[0, 0] = v_t
    vt_ref[0] = v_t.astype(BF16)
    logf = _log_sigmoid(fl_ref[...] + bf_ref[...])
    logft_ref[0] = logf.T[0:N_HEADS, :]
    ti = lax.broadcasted_iota(jnp.int32, (tc, tc), 0)
    si = lax.broadcasted_iota(jnp.int32, (tc, tc), 1)
    c = _mm_exact_rhs((si <= ti).astype(BF16), logf) + carry_ref[...]
    carry_ref[...] = c[tc - 1:tc, :]
    c2 = c * LOG2E
    c_ref[...] = c2
    pieces = jnp.concatenate(_split3(-c2), axis=1)
    kaug = (jnp.dot(k.astype(BF16), pk_ref[...], preferred_element_type=F32)
            + jnp.dot(pieces, pc_ref[...], preferred_element_type=F32) + ones_ref[...])
    kaug_ref[...] = kaug.astype(BF16)


def _fox_prep(p_all, bf_pad, places, k_prev, v_prev, layer, *, n_seq, seq_len):
    tc = _tile(seq_len, (PREP_BLOCK, 256, 128))
    nb = seq_len // tc
    rows = n_seq * seq_len
    const = lambda a: pl.BlockSpec(a.shape, lambda b, j: (0, 0))
    rowblk = lambda n: pl.BlockSpec((tc, n), lambda b, j: (b * nb + j, 0))
    stacked = pl.BlockSpec((1, 1, W_H, tc), lambda b, j: (layer, b, 0, j))
    return pl.pallas_call(
        _fox_prep_kernel,
        grid=(n_seq, nb),
        in_specs=[pl.BlockSpec((tc, W_H), lambda b, j: (b * nb + j, COL_K // W_H)),
                  pl.BlockSpec((tc, W_H), lambda b, j: (b * nb + j, COL_V // W_H)),
                  pl.BlockSpec((tc, LANES), lambda b, j: (b * nb + j, COL_FL // LANES)),
                  pl.BlockSpec((1, LANES), lambda b, j: (0, 0)),
                  const(places[0]), const(places[1]), const(places[2]),
                  pl.BlockSpec(memory_space=pl.ANY), pl.BlockSpec(memory_space=pl.ANY)],
        out_specs=[stacked, stacked,
                   pl.BlockSpec((1, N_HEADS, tc), lambda b, j: (b, 0, j)),
                   rowblk(LANES), rowblk(N_HEADS * LANES),
                   pl.BlockSpec((1, W_H, tc), lambda b, j: (b, 0, j))],
        out_shape=[jax.ShapeDtypeStruct(k_prev.shape, F32),
                   jax.ShapeDtypeStruct(v_prev.shape, F32),
                   jax.ShapeDtypeStruct((n_seq, N_HEADS, seq_len), F32),
                   jax.ShapeDtypeStruct((rows, LANES), F32),
                   jax.ShapeDtypeStruct((rows, N_HEADS * LANES), BF16),
                   jax.ShapeDtypeStruct((n_seq, W_H, seq_len), BF16)],
        scratch_shapes=[pltpu.VMEM((1, LANES), F32)],
        input_output_aliases={7: 0, 8: 1},
        compiler_params=_cparams(("parallel", "arbitrary")),
        name="fox_prep",
    )(p_all, p_all, p_all, bf_pad, *places, k_prev, v_prev)


def _fox_prompt_kernel(q_ref, c_ref, kaug_ref, vt_ref, o_ref, m_ref, l_ref, acc_ref, *, blk):
    hp = pl.program_id(1)
    i = pl.program_id(2)
    lane = lax.broadcasted_iota(jnp.int32, (1, LANES), 1)
    c_blk = c_ref[...]
    q_all = q_ref[...] * (SCALE * LOG2E)
    q_aug = []
    for hh in range(2):
        q_h = q_all if hh == 0 else pltpu.roll(q_all, HEAD_DIM, 1)
        cq = jnp.sum(jnp.where(lane == 2 * hp + hh, c_blk, 0.0), axis=-1, keepdims=True)
        hi, mid, lo = _split3(cq)
        aug = jnp.where(lane < HEAD_DIM, q_h,
                        jnp.where(lane == CQ_LANE0, hi.astype(F32),
                                  jnp.where(lane == CQ_LANE0 + 1, mid.astype(F32),
                                            jnp.where(lane == CQ_LANE0 + 2, lo.astype(F32),
                                                      jnp.where(lane < CK_LANE0 + 3, 1.0, 0.0)))))
        q_aug.append(aug.astype(BF16))

    m_ref[...] = jnp.full(m_ref.shape, -jnp.inf, F32)
    l_ref[...] = jnp.zeros(l_ref.shape, F32)
    acc_ref[...] = jnp.zeros(acc_ref.shape, F32)
    key_i = lax.broadcasted_iota(jnp.int32, (blk, blk), 0)
    qry_i = lax.broadcasted_iota(jnp.int32, (blk, blk), 1)
    causal = key_i <= qry_i

    def step(j, masked):
        start = pl.multiple_of(j * blk, blk)
        vt = vt_ref[0, :, pl.ds(start, blk)]
        sts = [lax.dot_general(kaug_ref[pl.ds(start, blk), hh * LANES:(hh + 1) * LANES], q_aug[hh],
                               (((1,), (1,)), ((), ())), preferred_element_type=F32)
               for hh in range(2)]
        for hh in range(2):
            st = sts[hh]
            if masked:
                st = jnp.where(causal, st, -jnp.inf)
            m_old = m_ref[hh]
            m_new = jnp.maximum(m_old, jnp.max(st, axis=0, keepdims=True))
            alpha = jnp.exp2(m_old - m_new)
            pt = jnp.exp2(st - m_new)
            l_ref[hh] = alpha * l_ref[hh] + jnp.sum(pt, axis=0, keepdims=True)
            m_ref[hh] = m_new
            acc_ref[hh] = alpha * acc_ref[hh] + jnp.dot(vt, pt.astype(BF16),
                                                        preferred_element_type=F32)

    def body(j, carry):
        step(j, False)
        return carry

    lax.fori_loop(0, i, body, 0)
    step(i, True)
    chan = lax.broadcasted_iota(jnp.int32, (LANES, 1), 0)
    out_t = jnp.where(chan < HEAD_DIM, acc_ref[0] / l_ref[0], acc_ref[1] / l_ref[1])
    o_ref[...] = out_t.T.astype(o_ref.dtype)


def _fox_prompt(p_all, c, kaug, v_t, *, n_seq, seq_len):
    blk = _tile(seq_len, (ATT_BLOCK, 256, 128))
    nq = seq_len // blk
    kern = functools.partial(_fox_prompt_kernel, blk=blk)
    return pl.pallas_call(
        kern,
        grid=(n_seq, N_PAIRS, nq),
        in_specs=[pl.BlockSpec((blk, LANES), lambda b, hp, i: (b * nq + i, COL_Q // LANES + hp)),
                  pl.BlockSpec((blk, LANES), lambda b, hp, i: (b * nq + i, 0)),
                  pl.BlockSpec((seq_len, KAUG_W), lambda b, hp, i: (b, hp)),
                  pl.BlockSpec((1, LANES, seq_len), lambda b, hp, i: (b, hp, 0))],
        out_specs=pl.BlockSpec((blk, LANES), lambda b, hp, i: (b * nq + i, hp)),
        out_shape=jax.ShapeDtypeStruct((n_seq * seq_len, W_H), BF16),
        scratch_shapes=[pltpu.VMEM((2, 1, blk), F32), pltpu.VMEM((2, 1, blk), F32),
                        pltpu.VMEM((2, LANES, blk), F32)],
        compiler_params=_cparams(("parallel", "parallel", "arbitrary")),
        name="fox_prompt",
    )(p_all, c, kaug, v_t)


def _fox_sample_kernel(q_ref, k_ref, v_ref, fl_ref, bf_ref, ckt_ref, cvt_ref, clt_ref,
                       kprev_ref, vprev_ref, o_ref, logf_ref, kall_ref, vall_ref, *, t_new, past):
    del kprev_ref, vprev_ref
    T = t_new
    kall_ref[0] = k_ref[...]
    vall_ref[0] = v_ref[...]
    logf = _log_sigmoid(fl_ref[...] + bf_ref[...])
    logf_ref[...] = logf
    logf_pad = jnp.concatenate([logf, jnp.zeros((LANES - T, LANES), F32)], axis=0)
    ti = lax.broadcasted_iota(jnp.int32, (LANES, LANES), 0)
    si = lax.broadcasted_iota(jnp.int32, (LANES, LANES), 1)
    cnew_pad = _mm_exact_rhs((si <= ti).astype(BF16), logf_pad)
    cnew_t = cnew_pad.T
    cnew = cnew_pad[0:T, :]
    suf = clt_ref[0, 0]
    pos = lax.broadcasted_iota(jnp.int32, suf.shape, 1)
    total = suf
    d = 1
    while d < past:
        total = total + jnp.where(pos + d < past, pltpu.roll(total, past - d, 1), 0.0)
        d *= 2
    suf = total - suf

    lane = lax.broadcasted_iota(jnp.int32, (T, LANES), 1)
    rowi = lax.broadcasted_iota(jnp.int32, (T, T), 0)
    coli = lax.broadcasted_iota(jnp.int32, (T, T), 1)
    for h in range(N_HEADS):
        sl = slice(h * HEAD_DIM, (h + 1) * HEAD_DIM)
        q = q_ref[:, sl] * SCALE
        cq = jnp.sum(jnp.where(lane == h, cnew, 0.0), axis=-1, keepdims=True)
        s_c = _mm(q, ckt_ref[0, 0, h]) + (cq + suf[h:h + 1, :])
        s_n = _mm_nt(q, k_ref[:, sl]) + (cq - cnew_t[h:h + 1, 0:T])
        s_n = jnp.where(coli <= rowi, s_n, -jnp.inf)
        m = jnp.maximum(jnp.max(s_c, axis=-1, keepdims=True), jnp.max(s_n, axis=-1, keepdims=True))
        p_c = jnp.exp(s_c - m)
        p_n = jnp.exp(s_n - m)
        l = jnp.sum(p_c, axis=-1, keepdims=True) + jnp.sum(p_n, axis=-1, keepdims=True)
        o = _mm_nt(p_c, cvt_ref[0, 0, h]) + _mm(p_n, v_ref[:, sl])
        o_ref[:, sl] = (o / l).astype(o_ref.dtype)


def _fox_sample(p_all, bf_pad, cache_kt, cache_vt, cache_logf_t, k_prev, v_prev, layer,
                *, row0, n_seq, t_new):
    past = cache_kt.shape[4]
    rb0 = row0 // t_new
    blk512 = lambda col: pl.BlockSpec((t_new, W_H), lambda b: (rb0 + b, col // W_H))
    cache_spec = pl.BlockSpec((1, 1, N_HEADS, HEAD_DIM, past), lambda b: (layer, b, 0, 0, 0))
    stacked = pl.BlockSpec((1, t_new, W_H), lambda b: (layer, b, 0))
    return pl.pallas_call(
        functools.partial(_fox_sample_kernel, t_new=t_new, past=past),
        grid=(n_seq,),
        in_specs=[blk512(COL_Q), blk512(COL_K), blk512(COL_V),
                  pl.BlockSpec((t_new, LANES), lambda b: (rb0 + b, COL_FL // LANES)),
                  pl.BlockSpec((1, LANES), lambda b: (0, 0)),
                  cache_spec, cache_spec,
                  pl.BlockSpec((1, 1, N_HEADS, past), lambda b: (layer, b, 0, 0)),
                  pl.BlockSpec(memory_space=pl.ANY), pl.BlockSpec(memory_space=pl.ANY)],
        out_specs=[pl.BlockSpec((t_new, W_H), lambda b: (b, 0)),
                   pl.BlockSpec((t_new, LANES), lambda b: (b, 0)), stacked, stacked],
        out_shape=[jax.ShapeDtypeStruct((n_seq * t_new, W_H), BF16),
                   jax.ShapeDtypeStruct((n_seq * t_new, LANES), F32),
                   jax.ShapeDtypeStruct(k_prev.shape, F32),
                   jax.ShapeDtypeStruct(v_prev.shape, F32)],
        input_output_aliases={8: 2, 9: 3},
        compiler_params=_cparams(("parallel",)),
        name="fox_sample",
    )(p_all, p_all, p_all, p_all, bf_pad, cache_kt, cache_vt, cache_logf_t, k_prev, v_prev)


def _merge_kernel(x_ref, oap_ref, oas_ref, obp_ref, obs_ref, ga_ref, gb_ref, pa_ref, pb_ref, wo_ref,
                  o_ref, *, n_prompt_tiles):
    is_sample = pl.program_id(0) >= n_prompt_tiles
    oa = jnp.where(is_sample, oas_ref[...], oap_ref[...])
    ob = jnp.where(is_sample, obs_ref[...], obp_ref[...])
    ma = jnp.dot(oa, pa_ref[...], preferred_element_type=F32)
    mb = jnp.dot(ob, pb_ref[...], preferred_element_type=F32)
    mix = _sigmoid(ga_ref[...]) * ma + _sigmoid(gb_ref[...]) * mb
    o_ref[...] = x_ref[...] + jnp.dot(mix.astype(BF16), wo_ref[...], preferred_element_type=F32)


def _merge(x, oa_p, oa_s, ob_p, ob_s, p_all, pa, pb, wo, tm):
    m, d = x.shape
    npt = oa_p.shape[0] // tm
    row = lambda n, col=0: pl.BlockSpec((tm, n), lambda i: (i, col))
    prompt = pl.BlockSpec((tm, W_H), lambda i: (jnp.minimum(i, npt - 1), 0))
    sample = pl.BlockSpec((tm, W_H), lambda i: (jnp.maximum(i - npt, 0), 0))
    full = lambda a: pl.BlockSpec(a.shape, lambda i: (0, 0))
    return pl.pallas_call(
        functools.partial(_merge_kernel, n_prompt_tiles=npt),
        grid=(m // tm,),
        in_specs=[row(d), prompt, sample, prompt, sample, row(d, COL_GA // d), row(d, COL_GB // d),
                  full(pa), full(pb), full(wo)],
        out_specs=row(d),
        out_shape=jax.ShapeDtypeStruct((m, d), F32),
        compiler_params=_cparams(("parallel",)),
        name="merge",
    )(x, oa_p, oa_s, ob_p, ob_s, p_all, p_all, pa, pb, wo)


def _ffn_kernel(x_ref, g_ref, wg_ref, wu_ref, wd_ref, fg_ref, o_ref, h_ref, acc_ref, *, final):
    f = pl.program_id(1)

    @pl.when(f == 0)
    def _():
        x = x_ref[...]
        y = x * lax.rsqrt(jnp.mean(x * x, axis=-1, keepdims=True) + EPS) * g_ref[...]
        h_ref[...] = y.astype(BF16)
        acc_ref[...] = x

    h = h_ref[...]
    gate = jnp.dot(h, wg_ref[...], preferred_element_type=F32)
    up = jnp.dot(h, wu_ref[...], preferred_element_type=F32)
    act = gate * _sigmoid(gate) * up
    acc_ref[...] += jnp.dot(act.astype(BF16), wd_ref[...], preferred_element_type=F32)

    @pl.when(f == pl.num_programs(1) - 1)
    def _():
        y = acc_ref[...]
        if final:
            y = y * lax.rsqrt(jnp.mean(y * y, axis=-1, keepdims=True) + EPS) * fg_ref[...]
        o_ref[...] = y


def _ffn(x, g, wg, wu, wd, fg, tm, tf, final):
    m, d = x.shape
    dff = wg.shape[1]
    return pl.pallas_call(
        functools.partial(_ffn_kernel, final=final),
        grid=(m // tm, dff // tf),
        in_specs=[pl.BlockSpec((tm, d), lambda i, f: (i, 0)),
                  pl.BlockSpec((1, d), lambda i, f: (0, 0)),
                  pl.BlockSpec((d, tf), lambda i, f: (0, f)),
                  pl.BlockSpec((d, tf), lambda i, f: (0, f)),
                  pl.BlockSpec((tf, d), lambda i, f: (f, 0)),
                  pl.BlockSpec((1, d), lambda i, f: (0, 0))],
        out_specs=pl.BlockSpec((tm, d), lambda i, f: (i, 0)),
        out_shape=jax.ShapeDtypeStruct((m, d), F32),
        scratch_shapes=[pltpu.VMEM((tm, d), BF16), pltpu.VMEM((tm, d), F32)],
        compiler_params=_cparams(("parallel", "arbitrary")),
        name="ffn",
    )(x, g, wg, wu, wd, fg)


def kernel(x_prompt, x_sample, cache_fox_k, cache_fox_v, cache_fox_logf, state_rwkv, state_shift,
           norm1_g, w_in, rwkv_mu, rwkv_w0, rwkv_w2, rwkv_a0, rwkv_a2, rwkv_g2, rwkv_k_k, rwkv_k_a,
           rwkv_r_k, rwkv_lnx_g, rwkv_lnx_b, fox_bf, p_a, p_b, w_out, norm2_g, w_gate, w_up, w_down,
           final_g):
    n_b, seq, d = x_prompt.shape
    n_db, t_new, _ = x_sample.shape
    depth = w_in.shape[0]
    d_ff = w_gate.shape[2]
    m_prompt = n_b * seq
    m_sample = n_db * t_new
    m_all = m_prompt + m_sample

    o_q = R_COLS
    o_fl = R_COLS + 3 * W_H
    o_g = o_fl + N_HEADS
    w_t = jnp.swapaxes(w_in, 1, 2)
    w_cat_t = jnp.concatenate(
        [w_t[:, o_g:o_g + 2 * d], w_t[:, o_q:o_q + 3 * W_H], w_t[:, 0:R_COLS],
         w_t[:, o_fl:o_fl + N_HEADS],
         jnp.zeros((depth, P_COLS - COL_FL - N_HEADS, d), w_in.dtype)], axis=1).astype(BF16)
    bf_pad = jnp.pad(fox_bf, ((0, 0), (0, LANES - N_HEADS)))
    pa_b, pb_b, wo_b = p_a.astype(BF16), p_b.astype(BF16), w_out.astype(BF16)
    wg_b, wu_b, wd_b = w_gate.astype(BF16), w_up.astype(BF16), w_down.astype(BF16)
    w2_b, a2_b, g2_b = rwkv_w2.astype(BF16), rwkv_a2.astype(BF16), rwkv_g2.astype(BF16)
    cache_kt = jnp.transpose(cache_fox_k, (0, 1, 3, 4, 2))
    cache_vt = jnp.transpose(cache_fox_v, (0, 1, 3, 4, 2))
    cache_logf_t = jnp.swapaxes(cache_fox_logf, 2, 3)
    places = _place_matrices()

    tm = _tile(math.gcd(m_prompt, m_sample))
    tm_w = _tile(m_all, (1536, 512, 256, 128, 64, 32, 16, 8))
    tn = _tile(P_COLS, (512,))
    tf = _tile(d_ff, (1408, 256, 128))
    x = jnp.concatenate([x_prompt.reshape(m_prompt, d), x_sample.reshape(m_sample, d)], axis=0)
    zero_shift = jnp.zeros((n_b, 1, R_COLS), F32)
    zero_state = jnp.zeros((n_b, N_HEADS, HEAD_DIM, HEAD_DIM), F32)
    n_sub = PROMPT_SUBCHUNKS if seq % (PROMPT_CHUNK * PROMPT_SUBCHUNKS) == 0 else 1

    outs = {k: [] for k in ('lfp', 'sp', 'shp', 'lfd', 'sd', 'shd')}
    kp = jnp.zeros((depth, n_b, W_H, seq), F32)
    vp = jnp.zeros((depth, n_b, W_H, seq), F32)
    kd = jnp.zeros((depth, m_sample, W_H), F32)
    vd = jnp.zeros((depth, m_sample, W_H), F32)
    for l in range(depth):
        row = lambda a: a[l].reshape(1, -1)
        lw = dict(mu=row(rwkv_mu), w0=row(rwkv_w0), w2=w2_b[l], a0=row(rwkv_a0), a2=a2_b[l],
                  g2=g2_b[l], k_k=row(rwkv_k_k), k_a=row(rwkv_k_a), r_k=row(rwkv_r_k),
                  lnx_g=row(rwkv_lnx_g), lnx_b=row(rwkv_lnx_b))
        p_all = _norm_matmul(x, row(norm1_g), w_cat_t[l], tm_w, tn)

        oa_p, s_p, sh_p = _rwkv(p_all, zero_shift, zero_state, lw, row0=0, n_seq=n_b,
                                seq_len=seq, chunk=PROMPT_CHUNK, n_sub=n_sub)
        oa_s, s_d, sh_d = _rwkv(p_all, state_shift[l], state_rwkv[l], lw, row0=m_prompt,
                                n_seq=n_db, seq_len=t_new, chunk=t_new, n_sub=1)

        kp, vp, logft_p, c_p, kaug, v_t = _fox_prep(p_all, bf_pad[l:l + 1], places, kp, vp, l,
                                                    n_seq=n_b, seq_len=seq)
        ob_p = _fox_prompt(p_all, c_p, kaug, v_t, n_seq=n_b, seq_len=seq)
        ob_s, logf_d, kd, vd = _fox_sample(p_all, bf_pad[l:l + 1], cache_kt, cache_vt, cache_logf_t,
                                           kd, vd, l, row0=m_prompt, n_seq=n_db, t_new=t_new)

        x = _merge(x, oa_p, oa_s, ob_p, ob_s, p_all, pa_b[l], pb_b[l], wo_b[l], tm)
        x = _ffn(x, row(norm2_g), wg_b[l], wu_b[l], wd_b[l], final_g.reshape(1, -1), tm, tf,
                 final=(l == depth - 1))

        outs['lfp'].append(logft_p)
        outs['lfd'].append(logf_d[:, :N_HEADS].reshape(n_db, t_new, N_HEADS))
        outs['sp'].append(s_p)
        outs['shp'].append(sh_p)
        outs['sd'].append(s_d)
        outs['shd'].append(sh_d)

    st = {k: jnp.stack(v) for k, v in outs.items()}
    prompt_heads = lambda a: jnp.transpose(a.reshape(depth, n_b, N_HEADS, HEAD_DIM, seq),
                                           (0, 1, 4, 2, 3))
    sample_heads = lambda a: a.reshape(depth, n_db, t_new, N_HEADS, HEAD_DIM)
    y_prompt = x[:m_prompt].reshape(n_b, seq, d)
    y_sample = x[m_prompt:].reshape(n_db, t_new, d)
    return (y_prompt, y_sample, prompt_heads(kp), prompt_heads(vp), jnp.swapaxes(st['lfp'], 2, 3),
            st['sp'], st['shp'], sample_heads(kd), sample_heads(vd), st['lfd'], st['sd'],
            st['shd'])
```

```python
import functools
import math

import numpy as np
import jax
import jax.numpy as jnp
from jax import lax
from jax.experimental import pallas as pl
from jax.experimental.pallas import tpu as pltpu

F32 = jnp.float32
BF16 = jnp.bfloat16

HEAD_DIM = 64
N_HEADS = 8
N_PAIRS = N_HEADS // 2
W_H = N_HEADS * HEAD_DIM
DECAY_LORA = 64
AAA_LORA = 64
GATE_LORA = 128
R_COLS = 3 * W_H + DECAY_LORA + AAA_LORA + GATE_LORA
EPS = 1e-6
GN_EPS = 64e-5
SCALE = HEAD_DIM ** -0.5
LOG2E = 1.4426950408889634

LANES = 128
VMEM_LIMIT_BYTES = 56 * 1024 * 1024

COL_GA = 0
COL_GB = 1024
COL_Q = 2048
COL_K = 2560
COL_V = 3072
COL_R = 3584
COL_FL = 5376
P_COLS = 5632
PROMPT_CHUNK = 64
PROMPT_SUBCHUNKS = 4
ATT_BLOCK = 512
PREP_BLOCK = 512
KAUG_W = 2 * LANES
CQ_LANE0 = HEAD_DIM
CK_LANE0 = HEAD_DIM + 3


def _cparams(sem):
    return pltpu.CompilerParams(dimension_semantics=sem, vmem_limit_bytes=VMEM_LIMIT_BYTES)


def _mm(a, b):
    return jnp.dot(a.astype(BF16), b.astype(BF16), preferred_element_type=F32)


def _mm_nt(a, b):
    return lax.dot_general(a.astype(BF16), b.astype(BF16), (((1,), (1,)), ((), ())),
                           preferred_element_type=F32)


def _mm_tn(a, b):
    return lax.dot_general(a.astype(BF16), b.astype(BF16), (((0,), (0,)), ((), ())),
                           preferred_element_type=F32)


def _split3(x):
    hi = x.astype(BF16)
    r = x - hi.astype(F32)
    mid = r.astype(BF16)
    lo = (r - mid.astype(F32)).astype(BF16)
    return hi, mid, lo


def _mm_exact_rhs(m01, x):
    hi, mid, lo = _split3(x)
    d = lambda p: jnp.dot(m01, p, preferred_element_type=F32)
    return d(hi) + d(mid) + d(lo)


def _log_sigmoid(z):
    return jnp.minimum(z, 0.0) - jnp.log1p(jnp.exp(-jnp.abs(z)))


def _sigmoid(z):
    return 1.0 / (1.0 + jnp.exp(-z))


def _tile(m, cands=(512, 256, 128, 64, 32, 16, 8)):
    for t in cands:
        if m % t == 0:
            return t
    raise ValueError(m)


def _norm_matmul_kernel(x_ref, g_ref, w_ref, o_ref, h_ref):
    @pl.when(pl.program_id(1) == 0)
    def _():
        x = x_ref[...]
        y = x * lax.rsqrt(jnp.mean(x * x, axis=-1, keepdims=True) + EPS) * g_ref[...]
        h_ref[...] = y.astype(BF16)

    o_ref[...] = lax.dot_general(h_ref[...], w_ref[...], (((1,), (1,)), ((), ())),
                                 preferred_element_type=F32)


def _norm_matmul(x, g, w_t, tm, tn):
    m, d = x.shape
    n = w_t.shape[0]
    return pl.pallas_call(
        _norm_matmul_kernel,
        grid=(m // tm, n // tn),
        in_specs=[pl.BlockSpec((tm, d), lambda i, j: (i, 0)),
                  pl.BlockSpec((1, d), lambda i, j: (0, 0)),
                  pl.BlockSpec((tn, d), lambda i, j: (j, 0))],
        out_specs=pl.BlockSpec((tm, tn), lambda i, j: (i, j)),
        out_shape=jax.ShapeDtypeStruct((m, n), F32),
        scratch_shapes=[pltpu.VMEM((tm, d), BF16)],
        compiler_params=_cparams(("parallel", "arbitrary")),
        name="in_proj",
    )(x, g, w_t)


def _rwkv_kernel(pr_ref, shift_ref, s0_ref, mu_ref, w0_ref, w2_ref, a0_ref, a2_ref, g2_ref,
                 kk_ref, ka_ref, rk_ref, lg_ref, lb_ref,
                 o_ref, s_out_ref, shift_out_ref, state_ref, prev_ref, *, chunk, n_sub, n_steps):
    c = pl.program_id(1)
    C = chunk
    R = n_sub * C
    PW = LANES
    subs = range(n_sub)
    pairs = range(N_PAIRS)
    units = [(s, p) for s in subs for p in pairs]
    uheads = [(s, p, hh) for (s, p) in units for hh in range(2)]

    @pl.when(c == 0)
    def _():
        state_ref[...] = jnp.zeros(state_ref.shape, F32)
        for h in range(N_HEADS):
            lo = (h % 2) * HEAD_DIM
            state_ref[h // 2, lo:lo + HEAD_DIM, lo:lo + HEAD_DIM] = s0_ref[0, h]
        prev_ref[...] = shift_ref[0]

    pr = pr_ref[...]
    row = lax.broadcasted_iota(jnp.int32, (R, 1), 0)
    shifted = jnp.where(row == 0, prev_ref[...], pltpu.roll(pr, 1, 0))
    u = pr + (shifted - pr) * mu_ref[...]
    prev_ref[...] = pr[R - 1:R, :]

    r = u[:, 0:W_H]
    k = u[:, W_H:2 * W_H]
    v = u[:, 2 * W_H:3 * W_H]
    wl = u[:, 3 * W_H:3 * W_H + DECAY_LORA]
    al = u[:, 3 * W_H + DECAY_LORA:3 * W_H + DECAY_LORA + AAA_LORA]
    gl = u[:, 3 * W_H + DECAY_LORA + AAA_LORA:R_COLS]

    w_raw = w0_ref[...] + _mm(jnp.tanh(wl), w2_ref[...])
    logw = -jnp.exp(_log_sigmoid(w_raw) - 0.5)
    a_sig = _sigmoid(a0_ref[...] + _mm(al, a2_ref[...]))
    gate = _mm(_sigmoid(gl), g2_ref[...])
    kk_all = k * kk_ref[...]
    k_new = k * (1.0 + (a_sig - 1.0) * ka_ref[...])

    ti = lax.broadcasted_iota(jnp.int32, (R, R), 0)
    si = lax.broadcasted_iota(jnp.int32, (R, R), 1)
    cum_mask = (si <= ti) & ((si // C) == (ti // C))
    g_cum = _mm_exact_rhs(cum_mask.astype(BF16), logw)
    e_incl = jnp.exp(g_cum)
    e_excl = jnp.exp(g_cum - logw)
    e_inv = jnp.exp(-g_cum)

    tci = lax.broadcasted_iota(jnp.int32, (C, C), 0)
    sci = lax.broadcasted_iota(jnp.int32, (C, C), 1)
    lower_incl = sci <= tci
    lower_strict = sci < tci

    bi = lax.broadcasted_iota(jnp.int32, (PW, PW), 0) // HEAD_DIM
    bj = lax.broadcasted_iota(jnp.int32, (PW, PW), 1) // HEAD_DIM
    blockdiag = bi == bj
    ones_bd = blockdiag.astype(BF16)

    def unit_cols(x):
        return {(s, p): x[s * C:(s + 1) * C, p * PW:(p + 1) * PW] for (s, p) in units}

    def head_sums(xs):
        x = jnp.concatenate([xs[un] for un in units], axis=0)
        s = jnp.dot(x.astype(BF16), ones_bd, preferred_element_type=F32)
        return {un: s[i * C:(i + 1) * C] for i, un in enumerate(units)}

    lane = lax.broadcasted_iota(jnp.int32, (1, PW), 1)
    in_h0 = lane < HEAD_DIM
    lane2 = lax.broadcasted_iota(jnp.int32, (1, 2 * PW), 1)
    in_h0_2 = (lane2 % PW) < HEAD_DIM

    r_u, k_u, v_u = unit_cols(r), unit_cols(k_new), unit_cols(v)
    kk_u = unit_cols(kk_all)
    asig_u = unit_cols(a_sig)
    ei_u, ee_u, ev_u = unit_cols(e_incl), unit_cols(e_excl), unit_cols(e_inv)
    pc_u = {(s, p): e_incl[(s + 1) * C - 1:(s + 1) * C, p * PW:(p + 1) * PW] for (s, p) in units}

    ss = head_sums({un: kk_u[un] * kk_u[un] for un in units})
    kk_u = {un: kk_u[un] * lax.rsqrt(ss[un] + 1e-12) for un in units}
    a_t = {un: -kk_u[un] * ee_u[un] for un in units}
    r_t = {un: r_u[un] * ei_u[un] for un in units}
    b_hat = {un: kk_u[un] * asig_u[un] * ev_u[un] for un in units}
    k_hat = {un: k_u[un] * ev_u[un] for un in units}
    b_til = {un: b_hat[un] * pc_u[un] for un in units}
    k_til = {un: k_hat[un] * pc_u[un] for un in units}

    left = {un: jnp.concatenate([a_t[un], r_t[un]], axis=0) for un in units}
    right = {un: jnp.concatenate([b_hat[un], k_hat[un]], axis=0) for un in units}
    tci2 = lax.broadcasted_iota(jnp.int32, (C, 2 * C), 0)
    sci2 = lax.broadcasted_iota(jnp.int32, (C, 2 * C), 1) % C
    strict2 = sci2 < tci2
    incl2 = sci2 <= tci2
    n_pow, a_rb, top, bot = {}, {}, {}, {}
    for (s, p, hh) in uheads:
        lm = jnp.where(in_h0 if hh == 0 else ~in_h0, left[s, p], 0.0)
        quad = _mm_nt(lm, right[s, p])
        top[s, p, hh] = jnp.where(strict2, quad[:C], 0.0)
        bot[s, p, hh] = jnp.where(incl2, quad[C:], 0.0)
        n_pow[s, p, hh] = top[s, p, hh][:, :C]
        a_rb[s, p, hh] = bot[s, p, hh][:, :C]

    def by_head(mats, un, x, mask):
        return jnp.where(mask, _mm(mats[un + (0,)], x), _mm(mats[un + (1,)], x))

    zv = {un: jnp.concatenate([jnp.zeros((C, PW), F32), v_u[un]], axis=0) for un in units}
    xs = {un: jnp.concatenate([a_t[un], by_head(top, un, zv[un], in_h0)], axis=1)
          for un in units}
    y_v = {un: by_head(bot, un, zv[un], in_h0) for un in units}
    n_levels = C.bit_length() - 1
    for lvl in range(n_levels):
        xs = {un: xs[un] + by_head(n_pow, un, xs[un], in_h0_2) for un in units}
        if lvl + 1 < n_levels:
            n_pow = {hd: _mm(n_pow[hd], n_pow[hd]) for hd in uheads}

    chained = n_sub > 1
    bk_til = {un: jnp.concatenate([b_til[un], k_til[un]], axis=0) for un in units}
    if chained:
        tr_m = {un: jnp.where(blockdiag, _mm_tn(xs[un][:, :PW], b_til[un]), 0.0) for un in units}
        tr_d = {un: jnp.where(blockdiag,
                              _mm_tn(jnp.concatenate([xs[un][:, PW:], v_u[un]], axis=0),
                                     bk_til[un]), 0.0) for un in units}

    st = [state_ref[p] for p in pairs]
    y_u = {}
    pending = None
    for s in list(subs) + [None]:
        if s is not None:
            if chained:
                chain = [_mm(st[p], tr_m[s, p]) + tr_d[s, p] for p in pairs]
            u_p = [_mm_nt(xs[s, p][:, :PW], st[p]) + xs[s, p][:, PW:] for p in pairs]
            rs_p = [_mm_nt(r_t[s, p], st[p]) for p in pairs]
            if not chained:
                chain = [jnp.where(blockdiag,
                                   _mm_tn(jnp.concatenate([u_p[p], v_u[s, p]], axis=0),
                                          bk_til[s, p]), 0.0) for p in pairs]
        if pending is not None:
            s0, u0, rs0 = pending
            for p in pairs:
                y_u[s0, p] = rs0[p] + by_head(a_rb, (s0, p), u0[p], in_h0) + y_v[s0, p]
        if s is not None:
            st = [st[p] * pc_u[s, p] + chain[p] for p in pairs]
            pending = (s, u_p, rs_p)
    for p in pairs:
        state_ref[p] = st[p]

    inv_n = 1.0 / HEAD_DIM
    mu_y = head_sums(y_u)
    yc = {un: y_u[un] - mu_y[un] * inv_n for un in units}
    var = head_sums({un: yc[un] * yc[un] for un in units})
    rk_row = rk_ref[...]
    bon = head_sums({(s, p): r_u[s, p] * k_u[s, p] * rk_row[:, p * PW:(p + 1) * PW]
                     for (s, p) in units})
    for (s, p) in units:
        cols = slice(p * PW, (p + 1) * PW)
        rows = slice(s * C, (s + 1) * C)
        yn = yc[s, p] * lax.rsqrt(var[s, p] * inv_n + GN_EPS) * lg_ref[:, cols] + lb_ref[:, cols]
        o_ref[rows, cols] = ((yn + bon[s, p] * v_u[s, p]) * gate[rows, cols]).astype(o_ref.dtype)

    @pl.when(c == n_steps - 1)
    def _():
        for h in range(N_HEADS):
            lo = (h % 2) * HEAD_DIM
            s_out_ref[0, h] = state_ref[h // 2, lo:lo + HEAD_DIM, lo:lo + HEAD_DIM]
        shift_out_ref[0] = pr[R - 1:R, :]


def _rwkv(p_all, shift_in, s0, lw, *, row0, n_seq, seq_len, chunk, n_sub):
    rows = chunk * n_sub
    n_steps = seq_len // rows
    rb0 = row0 // rows
    vec = lambda n: pl.BlockSpec((1, n), lambda b, c: (0, 0))
    mat = lambda k, n: pl.BlockSpec((k, n), lambda b, c: (0, 0))
    kern = functools.partial(_rwkv_kernel, chunk=chunk, n_sub=n_sub, n_steps=n_steps)
    state_spec = pl.BlockSpec((1, N_HEADS, HEAD_DIM, HEAD_DIM), lambda b, c: (b, 0, 0, 0))
    shift_spec = pl.BlockSpec((1, 1, R_COLS), lambda b, c: (b, 0, 0))
    return pl.pallas_call(
        kern,
        grid=(n_seq, n_steps),
        in_specs=[
            pl.BlockSpec((rows, R_COLS), lambda b, c: (rb0 + b * n_steps + c, COL_R // R_COLS)),
            shift_spec, state_spec,
            vec(R_COLS), vec(W_H), mat(DECAY_LORA, W_H), vec(W_H), mat(AAA_LORA, W_H),
            mat(GATE_LORA, W_H), vec(W_H), vec(W_H), vec(W_H), vec(W_H), vec(W_H)],
        out_specs=[pl.BlockSpec((rows, W_H), lambda b, c: (b * n_steps + c, 0)),
                   state_spec, shift_spec],
        out_shape=[jax.ShapeDtypeStruct((n_seq * seq_len, W_H), BF16),
                   jax.ShapeDtypeStruct((n_seq, N_HEADS, HEAD_DIM, HEAD_DIM), F32),
                   jax.ShapeDtypeStruct((n_seq, 1, R_COLS), F32)],
        scratch_shapes=[pltpu.VMEM((N_PAIRS, LANES, LANES), F32),
                        pltpu.VMEM((1, R_COLS), F32)],
        compiler_params=_cparams(("parallel", "arbitrary")),
        name=f"rwkv_c{chunk}",
    )(p_all, shift_in, s0, lw['mu'], lw['w0'], lw['w2'], lw['a0'], lw['a2'], lw['g2'],
      lw['k_k'], lw['k_a'], lw['r_k'], lw['lnx_g'], lw['lnx_b'])


def _place_matrices():
    pk = np.zeros((W_H, N_HEADS * LANES), np.float32)
    pc = np.zeros((3 * LANES, N_HEADS * LANES), np.float32)
    ones = np.zeros((1, N_HEADS * LANES), np.float32)
    for h in range(N_HEADS):
        for j in range(HEAD_DIM):
            pk[h * HEAD_DIM + j, h * LANES + j] = 1.0
        for piece in range(3):
            pc[piece * LANES + h, h * LANES + CK_LANE0 + piece] = 1.0
            ones[0, h * LANES + CQ_LANE0 + piece] = 1.0
    return jnp.asarray(pk, BF16), jnp.asarray(pc, BF16), jnp.asarray(ones, F32)


def _fox_prep_kernel(k_ref, v_ref, fl_ref, bf_ref, pk_ref, pc_ref, ones_ref, kprev_ref, vprev_ref,
                     kall_ref, vall_ref, logft_ref, c_ref, kaug_ref, vt_ref, carry_ref):
    del kprev_ref, vprev_ref

    @pl.when(pl.program_id(1) == 0)
    def _():
        carry_ref[...] = jnp.zeros_like(carry_ref)

    tc = fl_ref.shape[0]
    k = k_ref[...]
    v_t = v_ref[...].T
    kall_ref[0, 0] = k.T
    vall_ref[0, 0] = v_t
    ones_half = jnp.ones((HEAD_DIM, tc), F32)
    for h in range(N_HEADS):
        vh = v_t[h * HEAD_DIM:(h + 1) * HEAD_DIM]
        grp = [vh, ones_half] if h % 2 == 0 else [ones_half, vh]
        vt_ref[0, h * LANES:(h + 1) * LANES, :] = jnp.concatenate(grp, axis=0).astype(BF16)
    logf = _log_sigmoid(fl_ref[...] + bf_ref[...])
    logft_ref[0] = logf.T[0:N_HEADS, :]
    ti = lax.broadcasted_iota(jnp.int32, (tc, tc), 0)
    si = lax.broadcasted_iota(jnp.int32, (tc, tc), 1)
    c = _mm_exact_rhs((si <= ti).astype(BF16), logf) + carry_ref[...]
    carry_ref[...] = c[tc - 1:tc, :]
    c2 = c * LOG2E
    c_ref[...] = c2
    pieces = jnp.concatenate(_split3(-c2), axis=1)
    kaug = (jnp.dot(k.astype(BF16), pk_ref[...], preferred_element_type=F32)
            + jnp.dot(pieces, pc_ref[...], preferred_element_type=F32) + ones_ref[...])
    kaug_ref[...] = kaug.astype(BF16)


def _fox_prep(p_all, bf_pad, places, k_prev, v_prev, layer, *, n_seq, seq_len):
    tc = _tile(seq_len, (PREP_BLOCK, 256, 128))
    nb = seq_len // tc
    rows = n_seq * seq_len
    const = lambda a: pl.BlockSpec(a.shape, lambda b, j: (0, 0))
    rowblk = lambda n: pl.BlockSpec((tc, n), lambda b, j: (b * nb + j, 0))
    stacked = pl.BlockSpec((1, 1, W_H, tc), lambda b, j: (layer, b, 0, j))
    return pl.pallas_call(
        _fox_prep_kernel,
        grid=(n_seq, nb),
        in_specs=[pl.BlockSpec((tc, W_H), lambda b, j: (b * nb + j, COL_K // W_H)),
                  pl.BlockSpec((tc, W_H), lambda b, j: (b * nb + j, COL_V // W_H)),
                  pl.BlockSpec((tc, LANES), lambda b, j: (b * nb + j, COL_FL // LANES)),
                  pl.BlockSpec((1, LANES), lambda b, j: (0, 0)),
                  const(places[0]), const(places[1]), const(places[2]),
                  pl.BlockSpec(memory_space=pl.ANY), pl.BlockSpec(memory_space=pl.ANY)],
        out_specs=[stacked, stacked,
                   pl.BlockSpec((1, N_HEADS, tc), lambda b, j: (b, 0, j)),
                   rowblk(LANES), rowblk(N_HEADS * LANES),
                   pl.BlockSpec((1, N_HEADS * LANES, tc), lambda b, j: (b, 0, j))],
        out_shape=[jax.ShapeDtypeStruct(k_prev.shape, F32),
                   jax.ShapeDtypeStruct(v_prev.shape, F32),
                   jax.ShapeDtypeStruct((n_seq, N_HEADS, seq_len), F32),
                   jax.ShapeDtypeStruct((rows, LANES), F32),
                   jax.ShapeDtypeStruct((rows, N_HEADS * LANES), BF16),
                   jax.ShapeDtypeStruct((n_seq, N_HEADS * LANES, seq_len), BF16)],
        scratch_shapes=[pltpu.VMEM((1, LANES), F32)],
        input_output_aliases={7: 0, 8: 1},
        compiler_params=_cparams(("parallel", "arbitrary")),
        name="fox_prep",
    )(p_all, p_all, p_all, bf_pad, *places, k_prev, v_prev)


def _fox_prompt_kernel(q_ref, c_ref, kaug_ref, vt_ref, o_ref, m_ref, acc_ref, sa_ref, sb_ref,
                       *, blk):
    hp = pl.program_id(1)
    i = pl.program_id(2)
    lane = lax.broadcasted_iota(jnp.int32, (1, LANES), 1)
    c_blk = c_ref[...]
    q_all = q_ref[...] * (SCALE * LOG2E)
    q_aug = []
    for hh in range(2):
        q_h = q_all if hh == 0 else pltpu.roll(q_all, HEAD_DIM, 1)
        cq = jnp.sum(jnp.where(lane == 2 * hp + hh, c_blk, 0.0), axis=-1, keepdims=True)
        hi, mid, lo = _split3(cq)
        aug = jnp.where(lane < HEAD_DIM, q_h,
                        jnp.where(lane == CQ_LANE0, hi.astype(F32),
                                  jnp.where(lane == CQ_LANE0 + 1, mid.astype(F32),
                                            jnp.where(lane == CQ_LANE0 + 2, lo.astype(F32),
                                                      jnp.where(lane < CK_LANE0 + 3, 1.0, 0.0)))))
        q_aug.append(aug.astype(BF16))

    m_ref[...] = jnp.full(m_ref.shape, -jnp.inf, F32)
    acc_ref[...] = jnp.zeros(acc_ref.shape, F32)
    key_i = lax.broadcasted_iota(jnp.int32, (blk, blk), 0)
    qry_i = lax.broadcasted_iota(jnp.int32, (blk, blk), 1)
    causal = key_i <= qry_i

    def scores(j, dst_ref):
        start = pl.multiple_of(j * blk, blk)
        for hh in range(2):
            dst_ref[hh] = lax.dot_general(
                kaug_ref[pl.ds(start, blk), hh * LANES:(hh + 1) * LANES], q_aug[hh],
                (((1,), (1,)), ((), ())), preferred_element_type=F32)

    def softmax_update(j, src_ref, masked):
        start = pl.multiple_of(j * blk, blk)
        for hh in range(2):
            vt = vt_ref[0, hh * LANES:(hh + 1) * LANES, pl.ds(start, blk)]
            st = src_ref[hh]
            if masked:
                st = jnp.where(causal, st, -jnp.inf)
            m_old = m_ref[hh]
            m_new = jnp.maximum(m_old, jnp.max(st, axis=0, keepdims=True))
            alpha = jnp.exp2(m_old - m_new)
            pt = jnp.exp2(st - m_new).astype(BF16)
            m_ref[hh] = m_new
            acc_ref[hh] = alpha * acc_ref[hh] + jnp.dot(vt, pt, preferred_element_type=F32)

    scores(0, sa_ref)

    def body(t, carry):
        j = 2 * t
        scores(j + 1, sb_ref)
        softmax_update(j, sa_ref, False)
        scores(j + 2, sa_ref)
        softmax_update(j + 1, sb_ref, False)
        return carry

    lax.fori_loop(0, i // 2, body, 0)

    @pl.when(i % 2 == 0)
    def _():
        softmax_update(i, sa_ref, True)

    @pl.when(i % 2 == 1)
    def _():
        scores(i, sb_ref)
        softmax_update(i - 1, sa_ref, False)
        softmax_update(i, sb_ref, True)

    chan = lax.broadcasted_iota(jnp.int32, (LANES, 1), 0)
    acc0 = acc_ref[0]
    acc1 = acc_ref[1]
    out_t = jnp.where(chan < HEAD_DIM, acc0 / acc0[HEAD_DIM:HEAD_DIM + 1, :], acc1 / acc1[0:1, :])
    o_ref[...] = out_t.T.astype(o_ref.dtype)


def _fox_prompt(p_all, c, kaug, v_t, *, n_seq, seq_len):
    blk = _tile(seq_len, (ATT_BLOCK, 256, 128))
    nq = seq_len // blk
    kern = functools.partial(_fox_prompt_kernel, blk=blk)
    return pl.pallas_call(
        kern,
        grid=(n_seq, N_PAIRS, nq),
        in_specs=[pl.BlockSpec((blk, LANES), lambda b, hp, i: (b * nq + i, COL_Q // LANES + hp)),
                  pl.BlockSpec((blk, LANES), lambda b, hp, i: (b * nq + i, 0)),
                  pl.BlockSpec((seq_len, KAUG_W), lambda b, hp, i: (b, hp)),
                  pl.BlockSpec((1, 2 * LANES, seq_len), lambda b, hp, i: (b, hp, 0))],
        out_specs=pl.BlockSpec((blk, LANES), lambda b, hp, i: (b * nq + i, hp)),
        out_shape=jax.ShapeDtypeStruct((n_seq * seq_len, W_H), BF16),
        scratch_shapes=[pltpu.VMEM((2, 1, blk), F32), pltpu.VMEM((2, LANES, blk), F32),
                        pltpu.VMEM((2, blk, blk), F32), pltpu.VMEM((2, blk, blk), F32)],
        compiler_params=_cparams(("parallel", "parallel", "arbitrary")),
        name="fox_prompt",
    )(p_all, c, kaug, v_t)


def _fox_sample_kernel(q_ref, k_ref, v_ref, fl_ref, bf_ref, ckt_ref, cvt_ref, clt_ref,
                       kprev_ref, vprev_ref, o_ref, logf_ref, kall_ref, vall_ref, *, t_new, past):
    del kprev_ref, vprev_ref
    T = t_new
    kall_ref[0] = k_ref[...]
    vall_ref[0] = v_ref[...]
    logf = _log_sigmoid(fl_ref[...] + bf_ref[...])
    logf_ref[...] = logf
    logf_pad = jnp.concatenate([logf, jnp.zeros((LANES - T, LANES), F32)], axis=0)
    ti = lax.broadcasted_iota(jnp.int32, (LANES, LANES), 0)
    si = lax.broadcasted_iota(jnp.int32, (LANES, LANES), 1)
    cnew_pad = _mm_exact_rhs((si <= ti).astype(BF16), logf_pad)
    cnew_t = cnew_pad.T
    cnew = cnew_pad[0:T, :]
    suf = clt_ref[0, 0]
    pos = lax.broadcasted_iota(jnp.int32, suf.shape, 1)
    total = suf
    d = 1
    while d < past:
        total = total + jnp.where(pos + d < past, pltpu.roll(total, past - d, 1), 0.0)
        d *= 2
    suf = total - suf

    lane = lax.broadcasted_iota(jnp.int32, (T, LANES), 1)
    rowi = lax.broadcasted_iota(jnp.int32, (T, T), 0)
    coli = lax.broadcasted_iota(jnp.int32, (T, T), 1)
    for h in range(N_HEADS):
        sl = slice(h * HEAD_DIM, (h + 1) * HEAD_DIM)
        q = q_ref[:, sl] * SCALE
        cq = jnp.sum(jnp.where(lane == h, cnew, 0.0), axis=-1, keepdims=True)
        s_c = _mm(q, ckt_ref[0, 0, h]) + (cq + suf[h:h + 1, :])
        s_n = _mm_nt(q, k_ref[:, sl]) + (cq - cnew_t[h:h + 1, 0:T])
        s_n = jnp.where(coli <= rowi, s_n, -jnp.inf)
        m = jnp.maximum(jnp.max(s_c, axis=-1, keepdims=True), jnp.max(s_n, axis=-1, keepdims=True))
        p_c = jnp.exp(s_c - m)
        p_n = jnp.exp(s_n - m)
        l = jnp.sum(p_c, axis=-1, keepdims=True) + jnp.sum(p_n, axis=-1, keepdims=True)
        o = _mm_nt(p_c, cvt_ref[0, 0, h]) + _mm(p_n, v_ref[:, sl])
        o_ref[:, sl] = (o / l).astype(o_ref.dtype)


def _fox_sample(p_all, bf_pad, cache_kt, cache_vt, cache_logf_t, k_prev, v_prev, layer,
                *, row0, n_seq, t_new):
    past = cache_kt.shape[4]
    rb0 = row0 // t_new
    blk512 = lambda col: pl.BlockSpec((t_new, W_H), lambda b: (rb0 + b, col // W_H))
    cache_spec = pl.BlockSpec((1, 1, N_HEADS, HEAD_DIM, past), lambda b: (layer, b, 0, 0, 0))
    stacked = pl.BlockSpec((1, t_new, W_H), lambda b: (layer, b, 0))
    return pl.pallas_call(
        functools.partial(_fox_sample_kernel, t_new=t_new, past=past),
        grid=(n_seq,),
        in_specs=[blk512(COL_Q), blk512(COL_K), blk512(COL_V),
                  pl.BlockSpec((t_new, LANES), lambda b: (rb0 + b, COL_FL // LANES)),
                  pl.BlockSpec((1, LANES), lambda b: (0, 0)),
                  cache_spec, cache_spec,
                  pl.BlockSpec((1, 1, N_HEADS, past), lambda b: (layer, b, 0, 0)),
                  pl.BlockSpec(memory_space=pl.ANY), pl.BlockSpec(memory_space=pl.ANY)],
        out_specs=[pl.BlockSpec((t_new, W_H), lambda b: (b, 0)),
                   pl.BlockSpec((t_new, LANES), lambda b: (b, 0)), stacked, stacked],
        out_shape=[jax.ShapeDtypeStruct((n_seq * t_new, W_H), BF16),
                   jax.ShapeDtypeStruct((n_seq * t_new, LANES), F32),
                   jax.ShapeDtypeStruct(k_prev.shape, F32),
                   jax.ShapeDtypeStruct(v_prev.shape, F32)],
        input_output_aliases={8: 2, 9: 3},
        compiler_params=_cparams(("parallel",)),
        name="fox_sample",
    )(p_all, p_all, p_all, p_all, bf_pad, cache_kt, cache_vt, cache_logf_t, k_prev, v_prev)


def _merge_kernel(x_ref, oap_ref, oas_ref, obp_ref, obs_ref, ga_ref, gb_ref, pa_ref, pb_ref, wo_ref,
                  o_ref, *, n_prompt_tiles):
    is_sample = pl.program_id(0) >= n_prompt_tiles
    oa = jnp.where(is_sample, oas_ref[...], oap_ref[...])
    ob = jnp.where(is_sample, obs_ref[...], obp_ref[...])
    ma = jnp.dot(oa, pa_ref[...], preferred_element_type=F32)
    mb = jnp.dot(ob, pb_ref[...], preferred_element_type=F32)
    mix = _sigmoid(ga_ref[...]) * ma + _sigmoid(gb_ref[...]) * mb
    o_ref[...] = x_ref[...] + jnp.dot(mix.astype(BF16), wo_ref[...], preferred_element_type=F32)


def _merge(x, oa_p, oa_s, ob_p, ob_s, p_all, pa, pb, wo, tm):
    m, d = x.shape
    npt = oa_p.shape[0] // tm
    row = lambda n, col=0: pl.BlockSpec((tm, n), lambda i: (i, col))
    prompt = pl.BlockSpec((tm, W_H), lambda i: (jnp.minimum(i, npt - 1), 0))
    sample = pl.BlockSpec((tm, W_H), lambda i: (jnp.maximum(i - npt, 0), 0))
    full = lambda a: pl.BlockSpec(a.shape, lambda i: (0, 0))
    return pl.pallas_call(
        functools.partial(_merge_kernel, n_prompt_tiles=npt),
        grid=(m // tm,),
        in_specs=[row(d), prompt, sample, prompt, sample, row(d, COL_GA // d), row(d, COL_GB // d),
                  full(pa), full(pb), full(wo)],
        out_specs=row(d),
        out_shape=jax.ShapeDtypeStruct((m, d), F32),
        compiler_params=_cparams(("parallel",)),
        name="merge",
    )(x, oa_p, oa_s, ob_p, ob_s, p_all, p_all, pa, pb, wo)


def _ffn_kernel(x_ref, g_ref, wg_ref, wu_ref, wd_ref, fg_ref, o_ref, h_ref, acc_ref, *, final):
    f = pl.program_id(1)

    @pl.when(f == 0)
    def _():
        x = x_ref[...]
        y = x * lax.rsqrt(jnp.mean(x * x, axis=-1, keepdims=True) + EPS) * g_ref[...]
        h_ref[...] = y.astype(BF16)
        acc_ref[...] = x

    h = h_ref[...]
    gate = jnp.dot(h, wg_ref[...], preferred_element_type=F32)
    up = jnp.dot(h, wu_ref[...], preferred_element_type=F32)
    act = gate * _sigmoid(gate) * up
    acc_ref[...] += jnp.dot(act.astype(BF16), wd_ref[...], preferred_element_type=F32)

    @pl.when(f == pl.num_programs(1) - 1)
    def _():
        y = acc_ref[...]
        if final:
            y = y * lax.rsqrt(jnp.mean(y * y, axis=-1, keepdims=True) + EPS) * fg_ref[...]
        o_ref[...] = y


def _ffn(x, g, wg, wu, wd, fg, tm, tf, final):
    m, d = x.shape
    dff = wg.shape[1]
    return pl.pallas_call(
        functools.partial(_ffn_kernel, final=final),
        grid=(m // tm, dff // tf),
        in_specs=[pl.BlockSpec((tm, d), lambda i, f: (i, 0)),
                  pl.BlockSpec((1, d), lambda i, f: (0, 0)),
                  pl.BlockSpec((d, tf), lambda i, f: (0, f)),
                  pl.BlockSpec((d, tf), lambda i, f: (0, f)),
                  pl.BlockSpec((tf, d), lambda i, f: (f, 0)),
                  pl.BlockSpec((1, d), lambda i, f: (0, 0))],
        out_specs=pl.BlockSpec((tm, d), lambda i, f: (i, 0)),
        out_shape=jax.ShapeDtypeStruct((m, d), F32),
        scratch_shapes=[pltpu.VMEM((tm, d), BF16), pltpu.VMEM((tm, d), F32)],
        compiler_params=_cparams(("parallel", "arbitrary")),
        name="ffn",
    )(x, g, wg, wu, wd, fg)


def kernel(x_prompt, x_sample, cache_fox_k, cache_fox_v, cache_fox_logf, state_rwkv, state_shift,
           norm1_g, w_in, rwkv_mu, rwkv_w0, rwkv_w2, rwkv_a0, rwkv_a2, rwkv_g2, rwkv_k_k, rwkv_k_a,
           rwkv_r_k, rwkv_lnx_g, rwkv_lnx_b, fox_bf, p_a, p_b, w_out, norm2_g, w_gate, w_up, w_down,
           final_g):
    n_b, seq, d = x_prompt.shape
    n_db, t_new, _ = x_sample.shape
    depth = w_in.shape[0]
    d_ff = w_gate.shape[2]
    m_prompt = n_b * seq
    m_sample = n_db * t_new
    m_all = m_prompt + m_sample

    o_q = R_COLS
    o_fl = R_COLS + 3 * W_H
    o_g = o_fl + N_HEADS
    w_t = jnp.swapaxes(w_in, 1, 2)
    w_cat_t = jnp.concatenate(
        [w_t[:, o_g:o_g + 2 * d], w_t[:, o_q:o_q + 3 * W_H], w_t[:, 0:R_COLS],
         w_t[:, o_fl:o_fl + N_HEADS],
         jnp.zeros((depth, P_COLS - COL_FL - N_HEADS, d), w_in.dtype)], axis=1).astype(BF16)
    bf_pad = jnp.pad(fox_bf, ((0, 0), (0, LANES - N_HEADS)))
    pa_b, pb_b, wo_b = p_a.astype(BF16), p_b.astype(BF16), w_out.astype(BF16)
    wg_b, wu_b, wd_b = w_gate.astype(BF16), w_up.astype(BF16), w_down.astype(BF16)
    w2_b, a2_b, g2_b = rwkv_w2.astype(BF16), rwkv_a2.astype(BF16), rwkv_g2.astype(BF16)
    cache_kt = jnp.transpose(cache_fox_k, (0, 1, 3, 4, 2))
    cache_vt = jnp.transpose(cache_fox_v, (0, 1, 3, 4, 2))
    cache_logf_t = jnp.swapaxes(cache_fox_logf, 2, 3)
    places = _place_matrices()

    tm = _tile(math.gcd(m_prompt, m_sample))
    tm_w = _tile(m_all, (1536, 512, 256, 128, 64, 32, 16, 8))
    tn = _tile(P_COLS, (512,))
    tf = _tile(d_ff, (1408, 256, 128))
    x = jnp.concatenate([x_prompt.reshape(m_prompt, d), x_sample.reshape(m_sample, d)], axis=0)
    zero_shift = jnp.zeros((n_b, 1, R_COLS), F32)
    zero_state = jnp.zeros((n_b, N_HEADS, HEAD_DIM, HEAD_DIM), F32)
    n_sub = PROMPT_SUBCHUNKS if seq % (PROMPT_CHUNK * PROMPT_SUBCHUNKS) == 0 else 1

    outs = {k: [] for k in ('lfp', 'sp', 'shp', 'lfd', 'sd', 'shd')}
    kp = jnp.zeros((depth, n_b, W_H, seq), F32)
    vp = jnp.zeros((depth, n_b, W_H, seq), F32)
    kd = jnp.zeros((depth, m_sample, W_H), F32)
    vd = jnp.zeros((depth, m_sample, W_H), F32)
    for l in range(depth):
        row = lambda a: a[l].reshape(1, -1)
        lw = dict(mu=row(rwkv_mu), w0=row(rwkv_w0), w2=w2_b[l], a0=row(rwkv_a0), a2=a2_b[l],
                  g2=g2_b[l], k_k=row(rwkv_k_k), k_a=row(rwkv_k_a), r_k=row(rwkv_r_k),
                  lnx_g=row(rwkv_lnx_g), lnx_b=row(rwkv_lnx_b))
        p_all = _norm_matmul(x, row(norm1_g), w_cat_t[l], tm_w, tn)

        oa_p, s_p, sh_p = _rwkv(p_all, zero_shift, zero_state, lw, row0=0, n_seq=n_b,
                                seq_len=seq, chunk=PROMPT_CHUNK, n_sub=n_sub)
        oa_s, s_d, sh_d = _rwkv(p_all, state_shift[l], state_rwkv[l], lw, row0=m_prompt,
                                n_seq=n_db, seq_len=t_new, chunk=t_new, n_sub=1)

        kp, vp, logft_p, c_p, kaug, v_t = _fox_prep(p_all, bf_pad[l:l + 1], places, kp, vp, l,
                                                    n_seq=n_b, seq_len=seq)
        ob_p = _fox_prompt(p_all, c_p, kaug, v_t, n_seq=n_b, seq_len=seq)
        ob_s, logf_d, kd, vd = _fox_sample(p_all, bf_pad[l:l + 1], cache_kt, cache_vt, cache_logf_t,
                                           kd, vd, l, row0=m_prompt, n_seq=n_db, t_new=t_new)

        x = _merge(x, oa_p, oa_s, ob_p, ob_s, p_all, pa_b[l], pb_b[l], wo_b[l], tm)
        x = _ffn(x, row(norm2_g), wg_b[l], wu_b[l], wd_b[l], final_g.reshape(1, -1), tm, tf,
                 final=(l == depth - 1))

        outs['lfp'].append(logft_p)
        outs['lfd'].append(logf_d[:, :N_HEADS].reshape(n_db, t_new, N_HEADS))
        outs['sp'].append(s_p)
        outs['shp'].append(sh_p)
        outs['sd'].append(s_d)
        outs['shd'].append(sh_d)

    st = {k: jnp.stack(v) for k, v in outs.items()}
    prompt_heads = lambda a: jnp.transpose(a.reshape(depth, n_b, N_HEADS, HEAD_DIM, seq),
                                           (0, 1, 4, 2, 3))
    sample_heads = lambda a: a.reshape(depth, n_db, t_new, N_HEADS, HEAD_DIM)
    y_prompt = x[:m_prompt].reshape(n_b, seq, d)
    y_sample = x[m_prompt:].reshape(n_db, t_new, d)
    return (y_prompt, y_sample, prompt_heads(kp), prompt_heads(vp), jnp.swapaxes(st['lfp'], 2, 3),
            st['sp'], st['shp'], sample_heads(kd), sample_heads(vd), st['lfd'], st['sd'],
            st['shd'])
```

```python
import functools
import math

import numpy as np
import jax
import jax.numpy as jnp
from jax import lax
from jax.experimental import pallas as pl
from jax.experimental.pallas import tpu as pltpu

F32 = jnp.float32
BF16 = jnp.bfloat16

HEAD_DIM = 64
N_HEADS = 8
N_PAIRS = N_HEADS // 2
W_H = N_HEADS * HEAD_DIM
DECAY_LORA = 64
AAA_LORA = 64
GATE_LORA = 128
R_COLS = 3 * W_H + DECAY_LORA + AAA_LORA + GATE_LORA
EPS = 1e-6
GN_EPS = 64e-5
SCALE = HEAD_DIM ** -0.5
LOG2E = 1.4426950408889634

LANES = 128
VMEM_LIMIT_BYTES = 56 * 1024 * 1024

C16_GA = 0
C16_GB = 1024
C16_Q = 2048
P16_COLS = 2560
C32_R = 0
C32_K = 2048
C32_V = 2560
C32_FL = 3072
P32_COLS = 3584
PROMPT_CHUNK = 64
PROMPT_SUBCHUNKS = 4
ATT_BLOCK = 512
PREP_BLOCK = 512
KAUG_W = 2 * LANES
CQ_LANE0 = HEAD_DIM
CK_LANE0 = HEAD_DIM + 3


def _cparams(sem):
    return pltpu.CompilerParams(dimension_semantics=sem, vmem_limit_bytes=VMEM_LIMIT_BYTES)


def _mm(a, b):
    return jnp.dot(a.astype(BF16), b.astype(BF16), preferred_element_type=F32)


def _mm_nt(a, b):
    return lax.dot_general(a.astype(BF16), b.astype(BF16), (((1,), (1,)), ((), ())),
                           preferred_element_type=F32)


def _mm_tn(a, b):
    return lax.dot_general(a.astype(BF16), b.astype(BF16), (((0,), (0,)), ((), ())),
                           preferred_element_type=F32)


def _split3(x):
    hi = x.astype(BF16)
    r = x - hi.astype(F32)
    mid = r.astype(BF16)
    lo = (r - mid.astype(F32)).astype(BF16)
    return hi, mid, lo


def _mm_exact_rhs(m01, x):
    hi, mid, lo = _split3(x)
    d = lambda p: jnp.dot(m01, p, preferred_element_type=F32)
    return d(hi) + d(mid) + d(lo)


def _log_sigmoid(z):
    return jnp.minimum(z, 0.0) - jnp.log1p(jnp.exp(-jnp.abs(z)))


def _sigmoid(z):
    return 1.0 / (1.0 + jnp.exp(-z))


def _tile(m, cands=(512, 256, 128, 64, 32, 16, 8)):
    for t in cands:
        if m % t == 0:
            return t
    raise ValueError(m)


def _norm_matmul_kernel(x_ref, g_ref, w_ref, o16_ref, o32_ref, h_ref, *, n16):
    j = pl.program_id(1)

    @pl.when(j == 0)
    def _():
        x = x_ref[...]
        y = x * lax.rsqrt(jnp.mean(x * x, axis=-1, keepdims=True) + EPS) * g_ref[...]
        h_ref[...] = y.astype(BF16)

    acc = lax.dot_general(h_ref[...], w_ref[...], (((1,), (1,)), ((), ())),
                          preferred_element_type=F32)

    @pl.when(j < n16)
    def _():
        o16_ref[...] = acc.astype(BF16)

    @pl.when(j >= n16)
    def _():
        o32_ref[...] = acc


def _norm_matmul(x, g, w_t, tm, tn):
    m, d = x.shape
    n16 = P16_COLS // tn
    n32 = P32_COLS // tn
    return pl.pallas_call(
        functools.partial(_norm_matmul_kernel, n16=n16),
        grid=(m // tm, n16 + n32),
        in_specs=[pl.BlockSpec((tm, d), lambda i, j: (i, 0)),
                  pl.BlockSpec((1, d), lambda i, j: (0, 0)),
                  pl.BlockSpec((tn, d), lambda i, j: (j, 0))],
        out_specs=[pl.BlockSpec((tm, tn), lambda i, j: (i, jnp.minimum(j, n16 - 1))),
                   pl.BlockSpec((tm, tn), lambda i, j: (i, jnp.maximum(j - n16, 0)))],
        out_shape=[jax.ShapeDtypeStruct((m, P16_COLS), BF16),
                   jax.ShapeDtypeStruct((m, P32_COLS), F32)],
        scratch_shapes=[pltpu.VMEM((tm, d), BF16)],
        compiler_params=_cparams(("parallel", "arbitrary")),
        name="in_proj",
    )(x, g, w_t)


def _rwkv_kernel(pr_ref, shift_ref, s0_ref, mu_ref, w0_ref, w2_ref, a0_ref, a2_ref, g2_ref,
                 kk_ref, ka_ref, rk_ref, lg_ref, lb_ref,
                 o_ref, s_out_ref, shift_out_ref, state_ref, prev_ref, *, chunk, n_sub, n_steps):
    c = pl.program_id(1)
    C = chunk
    R = n_sub * C
    PW = LANES
    subs = range(n_sub)
    pairs = range(N_PAIRS)
    units = [(s, p) for s in subs for p in pairs]
    uheads = [(s, p, hh) for (s, p) in units for hh in range(2)]

    @pl.when(c == 0)
    def _():
        state_ref[...] = jnp.zeros(state_ref.shape, F32)
        for h in range(N_HEADS):
            lo = (h % 2) * HEAD_DIM
            state_ref[h // 2, lo:lo + HEAD_DIM, lo:lo + HEAD_DIM] = s0_ref[0, h]
        prev_ref[...] = shift_ref[0]

    pr = pr_ref[...]
    row = lax.broadcasted_iota(jnp.int32, (R, 1), 0)
    shifted = jnp.where(row == 0, prev_ref[...], pltpu.roll(pr, 1, 0))
    u = pr + (shifted - pr) * mu_ref[...]
    prev_ref[...] = pr[R - 1:R, :]

    r = u[:, 0:W_H]
    k = u[:, W_H:2 * W_H]
    v = u[:, 2 * W_H:3 * W_H]
    wl = u[:, 3 * W_H:3 * W_H + DECAY_LORA]
    al = u[:, 3 * W_H + DECAY_LORA:3 * W_H + DECAY_LORA + AAA_LORA]
    gl = u[:, 3 * W_H + DECAY_LORA + AAA_LORA:R_COLS]

    w_raw = w0_ref[...] + _mm(jnp.tanh(wl), w2_ref[...])
    logw = -jnp.exp(_log_sigmoid(w_raw) - 0.5)
    a_sig = _sigmoid(a0_ref[...] + _mm(al, a2_ref[...]))
    gate = _mm(_sigmoid(gl), g2_ref[...])
    kk_all = k * kk_ref[...]
    k_new = k * (1.0 + (a_sig - 1.0) * ka_ref[...])

    ti = lax.broadcasted_iota(jnp.int32, (R, R), 0)
    si = lax.broadcasted_iota(jnp.int32, (R, R), 1)
    cum_mask = (si <= ti) & ((si // C) == (ti // C))
    g_cum = _mm_exact_rhs(cum_mask.astype(BF16), logw)
    e_incl = jnp.exp(g_cum)
    e_excl = jnp.exp(g_cum - logw)
    e_inv = jnp.exp(-g_cum)

    tci = lax.broadcasted_iota(jnp.int32, (C, C), 0)
    sci = lax.broadcasted_iota(jnp.int32, (C, C), 1)
    lower_incl = sci <= tci
    lower_strict = sci < tci

    bi = lax.broadcasted_iota(jnp.int32, (PW, PW), 0) // HEAD_DIM
    bj = lax.broadcasted_iota(jnp.int32, (PW, PW), 1) // HEAD_DIM
    blockdiag = bi == bj
    ones_bd = blockdiag.astype(BF16)

    def unit_cols(x):
        return {(s, p): x[s * C:(s + 1) * C, p * PW:(p + 1) * PW] for (s, p) in units}

    def head_sums(xs):
        x = jnp.concatenate([xs[un] for un in units], axis=0)
        s = jnp.dot(x.astype(BF16), ones_bd, preferred_element_type=F32)
        return {un: s[i * C:(i + 1) * C] for i, un in enumerate(units)}

    lane = lax.broadcasted_iota(jnp.int32, (1, PW), 1)
    in_h0 = lane < HEAD_DIM
    lane2 = lax.broadcasted_iota(jnp.int32, (1, 2 * PW), 1)
    in_h0_2 = (lane2 % PW) < HEAD_DIM

    r_u, k_u, v_u = unit_cols(r), unit_cols(k_new), unit_cols(v)
    kk_u = unit_cols(kk_all)
    asig_u = unit_cols(a_sig)
    ei_u, ee_u, ev_u = unit_cols(e_incl), unit_cols(e_excl), unit_cols(e_inv)
    pc_u = {(s, p): e_incl[(s + 1) * C - 1:(s + 1) * C, p * PW:(p + 1) * PW] for (s, p) in units}

    ss = head_sums({un: kk_u[un] * kk_u[un] for un in units})
    kk_u = {un: kk_u[un] * lax.rsqrt(ss[un] + 1e-12) for un in units}
    a_t = {un: -kk_u[un] * ee_u[un] for un in units}
    r_t = {un: r_u[un] * ei_u[un] for un in units}
    b_hat = {un: kk_u[un] * asig_u[un] * ev_u[un] for un in units}
    k_hat = {un: k_u[un] * ev_u[un] for un in units}
    b_til = {un: b_hat[un] * pc_u[un] for un in units}
    k_til = {un: k_hat[un] * pc_u[un] for un in units}

    left = {un: jnp.concatenate([a_t[un], r_t[un]], axis=0) for un in units}
    right = {un: jnp.concatenate([b_hat[un], k_hat[un]], axis=0) for un in units}
    tci2 = lax.broadcasted_iota(jnp.int32, (C, 2 * C), 0)
    sci2 = lax.broadcasted_iota(jnp.int32, (C, 2 * C), 1) % C
    strict2 = sci2 < tci2
    incl2 = sci2 <= tci2
    n_pow, a_rb, top, bot = {}, {}, {}, {}
    for (s, p, hh) in uheads:
        lm = jnp.where(in_h0 if hh == 0 else ~in_h0, left[s, p], 0.0)
        quad = _mm_nt(lm, right[s, p])
        top[s, p, hh] = jnp.where(strict2, quad[:C], 0.0)
        bot[s, p, hh] = jnp.where(incl2, quad[C:], 0.0)
        n_pow[s, p, hh] = top[s, p, hh][:, :C]
        a_rb[s, p, hh] = bot[s, p, hh][:, :C]

    def by_head(mats, un, x, mask):
        return jnp.where(mask, _mm(mats[un + (0,)], x), _mm(mats[un + (1,)], x))

    zv = {un: jnp.concatenate([jnp.zeros((C, PW), F32), v_u[un]], axis=0) for un in units}
    xs = {un: jnp.concatenate([a_t[un], by_head(top, un, zv[un], in_h0)], axis=1)
          for un in units}
    y_v = {un: by_head(bot, un, zv[un], in_h0) for un in units}
    n_levels = C.bit_length() - 1
    for lvl in range(n_levels):
        xs = {un: xs[un] + by_head(n_pow, un, xs[un], in_h0_2) for un in units}
        if lvl + 1 < n_levels:
            n_pow = {hd: _mm(n_pow[hd], n_pow[hd]) for hd in uheads}

    chained = n_sub > 1
    bk_til = {un: jnp.concatenate([b_til[un], k_til[un]], axis=0) for un in units}
    if chained:
        tr_m = {un: jnp.where(blockdiag, _mm_tn(xs[un][:, :PW], b_til[un]), 0.0) for un in units}
        tr_d = {un: jnp.where(blockdiag,
                              _mm_tn(jnp.concatenate([xs[un][:, PW:], v_u[un]], axis=0),
                                     bk_til[un]), 0.0) for un in units}

    st = [state_ref[p] for p in pairs]
    y_u = {}
    pending = None
    for s in list(subs) + [None]:
        if s is not None:
            if chained:
                chain = [_mm(st[p], tr_m[s, p]) + tr_d[s, p] for p in pairs]
            u_p = [_mm_nt(xs[s, p][:, :PW], st[p]) + xs[s, p][:, PW:] for p in pairs]
            rs_p = [_mm_nt(r_t[s, p], st[p]) for p in pairs]
            if not chained:
                chain = [jnp.where(blockdiag,
                                   _mm_tn(jnp.concatenate([u_p[p], v_u[s, p]], axis=0),
                                          bk_til[s, p]), 0.0) for p in pairs]
        if pending is not None:
            s0, u0, rs0 = pending
            for p in pairs:
                y_u[s0, p] = rs0[p] + by_head(a_rb, (s0, p), u0[p], in_h0) + y_v[s0, p]
        if s is not None:
            st = [st[p] * pc_u[s, p] + chain[p] for p in pairs]
            pending = (s, u_p, rs_p)
    for p in pairs:
        state_ref[p] = st[p]

    inv_n = 1.0 / HEAD_DIM
    mu_y = head_sums(y_u)
    yc = {un: y_u[un] - mu_y[un] * inv_n for un in units}
    var = head_sums({un: yc[un] * yc[un] for un in units})
    rk_row = rk_ref[...]
    bon = head_sums({(s, p): r_u[s, p] * k_u[s, p] * rk_row[:, p * PW:(p + 1) * PW]
                     for (s, p) in units})
    for (s, p) in units:
        cols = slice(p * PW, (p + 1) * PW)
        rows = slice(s * C, (s + 1) * C)
        yn = yc[s, p] * lax.rsqrt(var[s, p] * inv_n + GN_EPS) * lg_ref[:, cols] + lb_ref[:, cols]
        o_ref[rows, cols] = ((yn + bon[s, p] * v_u[s, p]) * gate[rows, cols]).astype(o_ref.dtype)

    @pl.when(c == n_steps - 1)
    def _():
        for h in range(N_HEADS):
            lo = (h % 2) * HEAD_DIM
            s_out_ref[0, h] = state_ref[h // 2, lo:lo + HEAD_DIM, lo:lo + HEAD_DIM]
        shift_out_ref[0] = pr[R - 1:R, :]


def _rwkv(p_all, shift_in, s0, lw, *, row0, n_seq, seq_len, chunk, n_sub):
    rows = chunk * n_sub
    n_steps = seq_len // rows
    rb0 = row0 // rows
    vec = lambda n: pl.BlockSpec((1, n), lambda b, c: (0, 0))
    mat = lambda k, n: pl.BlockSpec((k, n), lambda b, c: (0, 0))
    kern = functools.partial(_rwkv_kernel, chunk=chunk, n_sub=n_sub, n_steps=n_steps)
    state_spec = pl.BlockSpec((1, N_HEADS, HEAD_DIM, HEAD_DIM), lambda b, c: (b, 0, 0, 0))
    shift_spec = pl.BlockSpec((1, 1, R_COLS), lambda b, c: (b, 0, 0))
    return pl.pallas_call(
        kern,
        grid=(n_seq, n_steps),
        in_specs=[
            pl.BlockSpec((rows, R_COLS), lambda b, c: (rb0 + b * n_steps + c, C32_R // R_COLS)),
            shift_spec, state_spec,
            vec(R_COLS), vec(W_H), mat(DECAY_LORA, W_H), vec(W_H), mat(AAA_LORA, W_H),
            mat(GATE_LORA, W_H), vec(W_H), vec(W_H), vec(W_H), vec(W_H), vec(W_H)],
        out_specs=[pl.BlockSpec((rows, W_H), lambda b, c: (b * n_steps + c, 0)),
                   state_spec, shift_spec],
        out_shape=[jax.ShapeDtypeStruct((n_seq * seq_len, W_H), BF16),
                   jax.ShapeDtypeStruct((n_seq, N_HEADS, HEAD_DIM, HEAD_DIM), F32),
                   jax.ShapeDtypeStruct((n_seq, 1, R_COLS), F32)],
        scratch_shapes=[pltpu.VMEM((N_PAIRS, LANES, LANES), F32),
                        pltpu.VMEM((1, R_COLS), F32)],
        compiler_params=_cparams(("parallel", "arbitrary")),
        name=f"rwkv_c{chunk}",
    )(p_all, shift_in, s0, lw['mu'], lw['w0'], lw['w2'], lw['a0'], lw['a2'], lw['g2'],
      lw['k_k'], lw['k_a'], lw['r_k'], lw['lnx_g'], lw['lnx_b'])


def _place_matrices():
    pk = np.zeros((W_H, N_HEADS * LANES), np.float32)
    pc = np.zeros((3 * LANES, N_HEADS * LANES), np.float32)
    ones = np.zeros((1, N_HEADS * LANES), np.float32)
    for h in range(N_HEADS):
        for j in range(HEAD_DIM):
            pk[h * HEAD_DIM + j, h * LANES + j] = 1.0
        for piece in range(3):
            pc[piece * LANES + h, h * LANES + CK_LANE0 + piece] = 1.0
            ones[0, h * LANES + CQ_LANE0 + piece] = 1.0
    return jnp.asarray(pk, BF16), jnp.asarray(pc, BF16), jnp.asarray(ones, F32)


def _fox_prep_kernel(k_ref, v_ref, fl_ref, bf_ref, pk_ref, pc_ref, ones_ref, kprev_ref, vprev_ref,
                     kall_ref, vall_ref, logft_ref, c_ref, kaug_ref, vt_ref, carry_ref):
    del kprev_ref, vprev_ref

    @pl.when(pl.program_id(1) == 0)
    def _():
        carry_ref[...] = jnp.zeros_like(carry_ref)

    tc = fl_ref.shape[0]
    k = k_ref[...]
    v_t = v_ref[...].T
    kall_ref[0, 0] = k.T
    vall_ref[0, 0] = v_t
    ones_half = jnp.ones((HEAD_DIM, tc), F32)
    for h in range(N_HEADS):
        vh = v_t[h * HEAD_DIM:(h + 1) * HEAD_DIM]
        grp = [vh, ones_half] if h % 2 == 0 else [ones_half, vh]
        vt_ref[0, h * LANES:(h + 1) * LANES, :] = jnp.concatenate(grp, axis=0).astype(BF16)
    logf = _log_sigmoid(fl_ref[...] + bf_ref[...])
    logft_ref[0] = logf.T[0:N_HEADS, :]
    ti = lax.broadcasted_iota(jnp.int32, (tc, tc), 0)
    si = lax.broadcasted_iota(jnp.int32, (tc, tc), 1)
    c = _mm_exact_rhs((si <= ti).astype(BF16), logf) + carry_ref[...]
    carry_ref[...] = c[tc - 1:tc, :]
    c2 = c * LOG2E
    c_ref[...] = c2
    pieces = jnp.concatenate(_split3(-c2), axis=1)
    kaug = (jnp.dot(k.astype(BF16), pk_ref[...], preferred_element_type=F32)
            + jnp.dot(pieces, pc_ref[...], preferred_element_type=F32) + ones_ref[...])
    kaug_ref[...] = kaug.astype(BF16)


def _fox_prep(p_all, bf_pad, places, k_prev, v_prev, layer, *, n_seq, seq_len):
    tc = _tile(seq_len, (PREP_BLOCK, 256, 128))
    nb = seq_len // tc
    rows = n_seq * seq_len
    const = lambda a: pl.BlockSpec(a.shape, lambda b, j: (0, 0))
    rowblk = lambda n: pl.BlockSpec((tc, n), lambda b, j: (b * nb + j, 0))
    stacked = pl.BlockSpec((1, 1, W_H, tc), lambda b, j: (layer, b, 0, j))
    return pl.pallas_call(
        _fox_prep_kernel,
        grid=(n_seq, nb),
        in_specs=[pl.BlockSpec((tc, W_H), lambda b, j: (b * nb + j, C32_K // W_H)),
                  pl.BlockSpec((tc, W_H), lambda b, j: (b * nb + j, C32_V // W_H)),
                  pl.BlockSpec((tc, LANES), lambda b, j: (b * nb + j, C32_FL // LANES)),
                  pl.BlockSpec((1, LANES), lambda b, j: (0, 0)),
                  const(places[0]), const(places[1]), const(places[2]),
                  pl.BlockSpec(memory_space=pl.ANY), pl.BlockSpec(memory_space=pl.ANY)],
        out_specs=[stacked, stacked,
                   pl.BlockSpec((1, N_HEADS, tc), lambda b, j: (b, 0, j)),
                   rowblk(LANES), rowblk(N_HEADS * LANES),
                   pl.BlockSpec((1, N_HEADS * LANES, tc), lambda b, j: (b, 0, j))],
        out_shape=[jax.ShapeDtypeStruct(k_prev.shape, F32),
                   jax.ShapeDtypeStruct(v_prev.shape, F32),
                   jax.ShapeDtypeStruct((n_seq, N_HEADS, seq_len), F32),
                   jax.ShapeDtypeStruct((rows, LANES), F32),
                   jax.ShapeDtypeStruct((rows, N_HEADS * LANES), BF16),
                   jax.ShapeDtypeStruct((n_seq, N_HEADS * LANES, seq_len), BF16)],
        scratch_shapes=[pltpu.VMEM((1, LANES), F32)],
        input_output_aliases={7: 0, 8: 1},
        compiler_params=_cparams(("parallel", "arbitrary")),
        name="fox_prep",
    )(p_all, p_all, p_all, bf_pad, *places, k_prev, v_prev)


def _fox_prompt_kernel(q_ref, c_ref, kaug_ref, vt_ref, o_ref, m_ref, acc_ref, sa_ref, sb_ref,
                       *, blk):
    hp = pl.program_id(1)
    i = pl.program_id(2)
    lane = lax.broadcasted_iota(jnp.int32, (1, LANES), 1)
    c_blk = c_ref[...]
    q_all = q_ref[...].astype(F32) * (SCALE * LOG2E)
    q_aug = []
    for hh in range(2):
        q_h = q_all if hh == 0 else pltpu.roll(q_all, HEAD_DIM, 1)
        cq = jnp.sum(jnp.where(lane == 2 * hp + hh, c_blk, 0.0), axis=-1, keepdims=True)
        hi, mid, lo = _split3(cq)
        aug = jnp.where(lane < HEAD_DIM, q_h,
                        jnp.where(lane == CQ_LANE0, hi.astype(F32),
                                  jnp.where(lane == CQ_LANE0 + 1, mid.astype(F32),
                                            jnp.where(lane == CQ_LANE0 + 2, lo.astype(F32),
                                                      jnp.where(lane < CK_LANE0 + 3, 1.0, 0.0)))))
        q_aug.append(aug.astype(BF16))

    m_ref[...] = jnp.full(m_ref.shape, -jnp.inf, F32)
    acc_ref[...] = jnp.zeros(acc_ref.shape, F32)
    key_i = lax.broadcasted_iota(jnp.int32, (blk, blk), 0)
    qry_i = lax.broadcasted_iota(jnp.int32, (blk, blk), 1)
    causal = key_i <= qry_i

    def scores(j, dst_ref):
        start = pl.multiple_of(j * blk, blk)
        for hh in range(2):
            dst_ref[hh] = lax.dot_general(
                kaug_ref[pl.ds(start, blk), hh * LANES:(hh + 1) * LANES], q_aug[hh],
                (((1,), (1,)), ((), ())), preferred_element_type=F32)

    def softmax_update(j, src_ref, masked):
        start = pl.multiple_of(j * blk, blk)
        for hh in range(2):
            vt = vt_ref[0, hh * LANES:(hh + 1) * LANES, pl.ds(start, blk)]
            st = src_ref[hh]
            if masked:
                st = jnp.where(causal, st, -jnp.inf)
            m_old = m_ref[hh]
            m_new = jnp.maximum(m_old, jnp.max(st, axis=0, keepdims=True))
            alpha = jnp.exp2(m_old - m_new)
            pt = jnp.exp2(st - m_new).astype(BF16)
            m_ref[hh] = m_new
            acc_ref[hh] = alpha * acc_ref[hh] + jnp.dot(vt, pt, preferred_element_type=F32)

    scores(0, sa_ref)

    def body(t, carry):
        j = 2 * t
        scores(j + 1, sb_ref)
        softmax_update(j, sa_ref, False)
        scores(j + 2, sa_ref)
        softmax_update(j + 1, sb_ref, False)
        return carry

    lax.fori_loop(0, i // 2, body, 0)

    @pl.when(i % 2 == 0)
    def _():
        softmax_update(i, sa_ref, True)

    @pl.when(i % 2 == 1)
    def _():
        scores(i, sb_ref)
        softmax_update(i - 1, sa_ref, False)
        softmax_update(i, sb_ref, True)

    chan = lax.broadcasted_iota(jnp.int32, (LANES, 1), 0)
    acc0 = acc_ref[0]
    acc1 = acc_ref[1]
    out_t = jnp.where(chan < HEAD_DIM, acc0 / acc0[HEAD_DIM:HEAD_DIM + 1, :], acc1 / acc1[0:1, :])
    o_ref[...] = out_t.T.astype(o_ref.dtype)


def _fox_prompt(p16, c, kaug, v_t, *, n_seq, seq_len):
    blk = _tile(seq_len, (ATT_BLOCK, 256, 128))
    nq = seq_len // blk
    kern = functools.partial(_fox_prompt_kernel, blk=blk)
    return pl.pallas_call(
        kern,
        grid=(n_seq, N_PAIRS, nq),
        in_specs=[pl.BlockSpec((blk, LANES), lambda b, hp, i: (b * nq + i, C16_Q // LANES + hp)),
                  pl.BlockSpec((blk, LANES), lambda b, hp, i: (b * nq + i, 0)),
                  pl.BlockSpec((seq_len, KAUG_W), lambda b, hp, i: (b, hp)),
                  pl.BlockSpec((1, 2 * LANES, seq_len), lambda b, hp, i: (b, hp, 0))],
        out_specs=pl.BlockSpec((blk, LANES), lambda b, hp, i: (b * nq + i, hp)),
        out_shape=jax.ShapeDtypeStruct((n_seq * seq_len, W_H), BF16),
        scratch_shapes=[pltpu.VMEM((2, 1, blk), F32), pltpu.VMEM((2, LANES, blk), F32),
                        pltpu.VMEM((2, blk, blk), F32), pltpu.VMEM((2, blk, blk), F32)],
        compiler_params=_cparams(("parallel", "parallel", "arbitrary")),
        name="fox_prompt",
    )(p16, c, kaug, v_t)


def _fox_sample_kernel(q_ref, k_ref, v_ref, fl_ref, bf_ref, ckt_ref, cvt_ref, clt_ref,
                       kprev_ref, vprev_ref, o_ref, logf_ref, kall_ref, vall_ref, *, t_new, past):
    del kprev_ref, vprev_ref
    T = t_new
    kall_ref[0] = k_ref[...]
    vall_ref[0] = v_ref[...]
    logf = _log_sigmoid(fl_ref[...] + bf_ref[...])
    logf_ref[...] = logf
    logf_pad = jnp.concatenate([logf, jnp.zeros((LANES - T, LANES), F32)], axis=0)
    ti = lax.broadcasted_iota(jnp.int32, (LANES, LANES), 0)
    si = lax.broadcasted_iota(jnp.int32, (LANES, LANES), 1)
    cnew_pad = _mm_exact_rhs((si <= ti).astype(BF16), logf_pad)
    cnew_t = cnew_pad.T
    cnew = cnew_pad[0:T, :]
    suf = clt_ref[0, 0]
    pos = lax.broadcasted_iota(jnp.int32, suf.shape, 1)
    total = suf
    d = 1
    while d < past:
        total = total + jnp.where(pos + d < past, pltpu.roll(total, past - d, 1), 0.0)
        d *= 2
    suf = total - suf

    lane = lax.broadcasted_iota(jnp.int32, (T, LANES), 1)
    rowi = lax.broadcasted_iota(jnp.int32, (T, T), 0)
    coli = lax.broadcasted_iota(jnp.int32, (T, T), 1)
    for h in range(N_HEADS):
        sl = slice(h * HEAD_DIM, (h + 1) * HEAD_DIM)
        q = q_ref[:, sl].astype(F32) * SCALE
        cq = jnp.sum(jnp.where(lane == h, cnew, 0.0), axis=-1, keepdims=True)
        s_c = _mm(q, ckt_ref[0, 0, h]) + (cq + suf[h:h + 1, :])
        s_n = _mm_nt(q, k_ref[:, sl]) + (cq - cnew_t[h:h + 1, 0:T])
        s_n = jnp.where(coli <= rowi, s_n, -jnp.inf)
        m = jnp.maximum(jnp.max(s_c, axis=-1, keepdims=True), jnp.max(s_n, axis=-1, keepdims=True))
        p_c = jnp.exp(s_c - m)
        p_n = jnp.exp(s_n - m)
        l = jnp.sum(p_c, axis=-1, keepdims=True) + jnp.sum(p_n, axis=-1, keepdims=True)
        o = _mm_nt(p_c, cvt_ref[0, 0, h]) + _mm(p_n, v_ref[:, sl])
        o_ref[:, sl] = (o / l).astype(o_ref.dtype)


def _fox_sample(p16, p32, bf_pad, cache_kt, cache_vt, cache_logf_t, k_prev, v_prev, layer,
                *, row0, n_seq, t_new):
    past = cache_kt.shape[4]
    rb0 = row0 // t_new
    blk512 = lambda col: pl.BlockSpec((t_new, W_H), lambda b: (rb0 + b, col // W_H))
    cache_spec = pl.BlockSpec((1, 1, N_HEADS, HEAD_DIM, past), lambda b: (layer, b, 0, 0, 0))
    stacked = pl.BlockSpec((1, t_new, W_H), lambda b: (layer, b, 0))
    return pl.pallas_call(
        functools.partial(_fox_sample_kernel, t_new=t_new, past=past),
        grid=(n_seq,),
        in_specs=[blk512(C16_Q), blk512(C32_K), blk512(C32_V),
                  pl.BlockSpec((t_new, LANES), lambda b: (rb0 + b, C32_FL // LANES)),
                  pl.BlockSpec((1, LANES), lambda b: (0, 0)),
                  cache_spec, cache_spec,
                  pl.BlockSpec((1, 1, N_HEADS, past), lambda b: (layer, b, 0, 0)),
                  pl.BlockSpec(memory_space=pl.ANY), pl.BlockSpec(memory_space=pl.ANY)],
        out_specs=[pl.BlockSpec((t_new, W_H), lambda b: (b, 0)),
                   pl.BlockSpec((t_new, LANES), lambda b: (b, 0)), stacked, stacked],
        out_shape=[jax.ShapeDtypeStruct((n_seq * t_new, W_H), BF16),
                   jax.ShapeDtypeStruct((n_seq * t_new, LANES), F32),
                   jax.ShapeDtypeStruct(k_prev.shape, F32),
                   jax.ShapeDtypeStruct(v_prev.shape, F32)],
        input_output_aliases={8: 2, 9: 3},
        compiler_params=_cparams(("parallel",)),
        name="fox_sample",
    )(p16, p32, p32, p32, bf_pad, cache_kt, cache_vt, cache_logf_t, k_prev, v_prev)


def _merge_kernel(x_ref, oap_ref, oas_ref, obp_ref, obs_ref, ga_ref, gb_ref, pa_ref, pb_ref, wo_ref,
                  o_ref, *, n_prompt_tiles):
    is_sample = pl.program_id(0) >= n_prompt_tiles
    oa = jnp.where(is_sample, oas_ref[...], oap_ref[...])
    ob = jnp.where(is_sample, obs_ref[...], obp_ref[...])
    ma = jnp.dot(oa, pa_ref[...], preferred_element_type=F32)
    mb = jnp.dot(ob, pb_ref[...], preferred_element_type=F32)
    mix = _sigmoid(ga_ref[...].astype(F32)) * ma + _sigmoid(gb_ref[...].astype(F32)) * mb
    o_ref[...] = x_ref[...] + jnp.dot(mix.astype(BF16), wo_ref[...], preferred_element_type=F32)


def _merge(x, oa_p, oa_s, ob_p, ob_s, p16, pa, pb, wo, tm):
    m, d = x.shape
    npt = oa_p.shape[0] // tm
    row = lambda n, col=0: pl.BlockSpec((tm, n), lambda i: (i, col))
    prompt = pl.BlockSpec((tm, W_H), lambda i: (jnp.minimum(i, npt - 1), 0))
    sample = pl.BlockSpec((tm, W_H), lambda i: (jnp.maximum(i - npt, 0), 0))
    full = lambda a: pl.BlockSpec(a.shape, lambda i: (0, 0))
    return pl.pallas_call(
        functools.partial(_merge_kernel, n_prompt_tiles=npt),
        grid=(m // tm,),
        in_specs=[row(d), prompt, sample, prompt, sample, row(d, C16_GA // d), row(d, C16_GB // d),
                  full(pa), full(pb), full(wo)],
        out_specs=row(d),
        out_shape=jax.ShapeDtypeStruct((m, d), F32),
        compiler_params=_cparams(("parallel",)),
        name="merge",
    )(x, oa_p, oa_s, ob_p, ob_s, p16, p16, pa, pb, wo)


def _ffn_kernel(x_ref, g_ref, wg_ref, wu_ref, wd_ref, fg_ref, o_ref, h_ref, acc_ref, *, final):
    f = pl.program_id(1)

    @pl.when(f == 0)
    def _():
        x = x_ref[...]
        y = x * lax.rsqrt(jnp.mean(x * x, axis=-1, keepdims=True) + EPS) * g_ref[...]
        h_ref[...] = y.astype(BF16)
        acc_ref[...] = x

    h = h_ref[...]
    gate = jnp.dot(h, wg_ref[...], preferred_element_type=F32)
    up = jnp.dot(h, wu_ref[...], preferred_element_type=F32)
    act = gate * _sigmoid(gate) * up
    acc_ref[...] += jnp.dot(act.astype(BF16), wd_ref[...], preferred_element_type=F32)

    @pl.when(f == pl.num_programs(1) - 1)
    def _():
        y = acc_ref[...]
        if final:
            y = y * lax.rsqrt(jnp.mean(y * y, axis=-1, keepdims=True) + EPS) * fg_ref[...]
        o_ref[...] = y


def _ffn(x, g, wg, wu, wd, fg, tm, tf, final):
    m, d = x.shape
    dff = wg.shape[1]
    return pl.pallas_call(
        functools.partial(_ffn_kernel, final=final),
        grid=(m // tm, dff // tf),
        in_specs=[pl.BlockSpec((tm, d), lambda i, f: (i, 0)),
                  pl.BlockSpec((1, d), lambda i, f: (0, 0)),
                  pl.BlockSpec((d, tf), lambda i, f: (0, f)),
                  pl.BlockSpec((d, tf), lambda i, f: (0, f)),
                  pl.BlockSpec((tf, d), lambda i, f: (f, 0)),
                  pl.BlockSpec((1, d), lambda i, f: (0, 0))],
        out_specs=pl.BlockSpec((tm, d), lambda i, f: (i, 0)),
        out_shape=jax.ShapeDtypeStruct((m, d), F32),
        scratch_shapes=[pltpu.VMEM((tm, d), BF16), pltpu.VMEM((tm, d), F32)],
        compiler_params=_cparams(("parallel", "arbitrary")),
        name="ffn",
    )(x, g, wg, wu, wd, fg)


def kernel(x_prompt, x_sample, cache_fox_k, cache_fox_v, cache_fox_logf, state_rwkv, state_shift,
           norm1_g, w_in, rwkv_mu, rwkv_w0, rwkv_w2, rwkv_a0, rwkv_a2, rwkv_g2, rwkv_k_k, rwkv_k_a,
           rwkv_r_k, rwkv_lnx_g, rwkv_lnx_b, fox_bf, p_a, p_b, w_out, norm2_g, w_gate, w_up, w_down,
           final_g):
    n_b, seq, d = x_prompt.shape
    n_db, t_new, _ = x_sample.shape
    depth = w_in.shape[0]
    d_ff = w_gate.shape[2]
    m_prompt = n_b * seq
    m_sample = n_db * t_new
    m_all = m_prompt + m_sample

    o_q = R_COLS
    o_fl = R_COLS + 3 * W_H
    o_g = o_fl + N_HEADS
    w_t = jnp.swapaxes(w_in, 1, 2)
    zrows = lambda n: jnp.zeros((depth, n, d), w_in.dtype)
    w_cat_t = jnp.concatenate(
        [w_t[:, o_g:o_g + 2 * d], w_t[:, o_q:o_q + W_H],
         w_t[:, 0:R_COLS], zrows(C32_K - R_COLS), w_t[:, o_q + W_H:o_q + 3 * W_H],
         w_t[:, o_fl:o_fl + N_HEADS], zrows(P32_COLS - C32_FL - N_HEADS)],
        axis=1).astype(BF16)
    bf_pad = jnp.pad(fox_bf, ((0, 0), (0, LANES - N_HEADS)))
    pa_b, pb_b, wo_b = p_a.astype(BF16), p_b.astype(BF16), w_out.astype(BF16)
    wg_b, wu_b, wd_b = w_gate.astype(BF16), w_up.astype(BF16), w_down.astype(BF16)
    w2_b, a2_b, g2_b = rwkv_w2.astype(BF16), rwkv_a2.astype(BF16), rwkv_g2.astype(BF16)
    cache_kt = jnp.transpose(cache_fox_k, (0, 1, 3, 4, 2))
    cache_vt = jnp.transpose(cache_fox_v, (0, 1, 3, 4, 2))
    cache_logf_t = jnp.swapaxes(cache_fox_logf, 2, 3)
    places = _place_matrices()

    tm = _tile(math.gcd(m_prompt, m_sample))
    tm_w = _tile(m_all, (1536, 512, 256, 128, 64, 32, 16, 8))
    tn = _tile(math.gcd(P16_COLS, P32_COLS), (512,))
    tm_f = _tile(m_all, (768, 512, 256, 128, 64, 32, 16, 8))
    tf = _tile(d_ff, (1408, 256, 128))
    x = jnp.concatenate([x_prompt.reshape(m_prompt, d), x_sample.reshape(m_sample, d)], axis=0)
    zero_shift = jnp.zeros((n_b, 1, R_COLS), F32)
    zero_state = jnp.zeros((n_b, N_HEADS, HEAD_DIM, HEAD_DIM), F32)
    n_sub = PROMPT_SUBCHUNKS if seq % (PROMPT_CHUNK * PROMPT_SUBCHUNKS) == 0 else 1

    outs = {k: [] for k in ('lfp', 'sp', 'shp', 'lfd', 'sd', 'shd')}
    kp = jnp.zeros((depth, n_b, W_H, seq), F32)
    vp = jnp.zeros((depth, n_b, W_H, seq), F32)
    kd = jnp.zeros((depth, m_sample, W_H), F32)
    vd = jnp.zeros((depth, m_sample, W_H), F32)
    for l in range(depth):
        row = lambda a: a[l].reshape(1, -1)
        lw = dict(mu=row(rwkv_mu), w0=row(rwkv_w0), w2=w2_b[l], a0=row(rwkv_a0), a2=a2_b[l],
                  g2=g2_b[l], k_k=row(rwkv_k_k), k_a=row(rwkv_k_a), r_k=row(rwkv_r_k),
                  lnx_g=row(rwkv_lnx_g), lnx_b=row(rwkv_lnx_b))
        p16, p32 = _norm_matmul(x, row(norm1_g), w_cat_t[l], tm_w, tn)

        oa_p, s_p, sh_p = _rwkv(p32, zero_shift, zero_state, lw, row0=0, n_seq=n_b,
                                seq_len=seq, chunk=PROMPT_CHUNK, n_sub=n_sub)
        oa_s, s_d, sh_d = _rwkv(p32, state_shift[l], state_rwkv[l], lw, row0=m_prompt,
                                n_seq=n_db, seq_len=t_new, chunk=t_new, n_sub=1)

        kp, vp, logft_p, c_p, kaug, v_t = _fox_prep(p32, bf_pad[l:l + 1], places, kp, vp, l,
                                                    n_seq=n_b, seq_len=seq)
        ob_p = _fox_prompt(p16, c_p, kaug, v_t, n_seq=n_b, seq_len=seq)
        ob_s, logf_d, kd, vd = _fox_sample(p16, p32, bf_pad[l:l + 1], cache_kt, cache_vt,
                                           cache_logf_t, kd, vd, l, row0=m_prompt, n_seq=n_db,
                                           t_new=t_new)

        x = _merge(x, oa_p, oa_s, ob_p, ob_s, p16, pa_b[l], pb_b[l], wo_b[l], tm)
        x = _ffn(x, row(norm2_g), wg_b[l], wu_b[l], wd_b[l], final_g.reshape(1, -1), tm_f, tf,
                 final=(l == depth - 1))

        outs['lfp'].append(logft_p)
        outs['lfd'].append(logf_d[:, :N_HEADS].reshape(n_db, t_new, N_HEADS))
        outs['sp'].append(s_p)
        outs['shp'].append(sh_p)
        outs['sd'].append(s_d)
        outs['shd'].append(sh_d)

    st = {k: jnp.stack(v) for k, v in outs.items()}
    prompt_heads = lambda a: jnp.transpose(a.reshape(depth, n_b, N_HEADS, HEAD_DIM, seq),
                                           (0, 1, 4, 2, 3))
    sample_heads = lambda a: a.reshape(depth, n_db, t_new, N_HEADS, HEAD_DIM)
    y_prompt = x[:m_prompt].reshape(n_b, seq, d)
    y_sample = x[m_prompt:].reshape(n_db, t_new, d)
    return (y_prompt, y_sample, prompt_heads(kp), prompt_heads(vp), jnp.swapaxes(st['lfp'], 2, 3),
            st['sp'], st['shp'], sample_heads(kd), sample_heads(vd), st['lfd'], st['sd'],
            st['shd'])
```

```python
import functools
import math

import numpy as np
import jax
import jax.numpy as jnp
from jax import lax
from jax.experimental import pallas as pl
from jax.experimental.pallas import tpu as pltpu

F32 = jnp.float32
BF16 = jnp.bfloat16

HEAD_DIM = 64
N_HEADS = 8
N_PAIRS = N_HEADS // 2
W_H = N_HEADS * HEAD_DIM
DECAY_LORA = 64
AAA_LORA = 64
GATE_LORA = 128
R_COLS = 3 * W_H + DECAY_LORA + AAA_LORA + GATE_LORA
EPS = 1e-6
GN_EPS = 64e-5
SCALE = HEAD_DIM ** -0.5
LOG2E = 1.4426950408889634

LANES = 128
VMEM_LIMIT_BYTES = 56 * 1024 * 1024

C16_GA = 0
C16_GB = 1024
C16_Q = 2048
C32_K = 2560
C32_V = 3072
C32_R = 3584
C32_FL = 5376
P_COLS = 5632
PROMPT_CHUNK = 64
PROMPT_SUBCHUNKS = 4
ATT_BLOCK = 512
PREP_BLOCK = 512
KAUG_W = 2 * LANES
CQ_LANE0 = HEAD_DIM
CK_LANE0 = HEAD_DIM + 3


def _cparams(sem):
    return pltpu.CompilerParams(dimension_semantics=sem, vmem_limit_bytes=VMEM_LIMIT_BYTES)


def _mm(a, b):
    return jnp.dot(a.astype(BF16), b.astype(BF16), preferred_element_type=F32)


def _mm_nt(a, b):
    return lax.dot_general(a.astype(BF16), b.astype(BF16), (((1,), (1,)), ((), ())),
                           preferred_element_type=F32)


def _mm_tn(a, b):
    return lax.dot_general(a.astype(BF16), b.astype(BF16), (((0,), (0,)), ((), ())),
                           preferred_element_type=F32)


def _split3(x):
    hi = x.astype(BF16)
    r = x - hi.astype(F32)
    mid = r.astype(BF16)
    lo = (r - mid.astype(F32)).astype(BF16)
    return hi, mid, lo


def _mm_exact_rhs(m01, x):
    hi, mid, lo = _split3(x)
    d = lambda p: jnp.dot(m01, p, preferred_element_type=F32)
    return d(hi) + d(mid) + d(lo)


def _log_sigmoid(z):
    return jnp.minimum(z, 0.0) - jnp.log1p(jnp.exp(-jnp.abs(z)))


def _sigmoid(z):
    return 1.0 / (1.0 + jnp.exp(-z))


def _tile(m, cands=(512, 256, 128, 64, 32, 16, 8)):
    for t in cands:
        if m % t == 0:
            return t
    raise ValueError(m)


def _norm_matmul_kernel(x_ref, g_ref, w_ref, o_ref, h_ref):
    @pl.when(pl.program_id(1) == 0)
    def _():
        x = x_ref[...]
        y = x * lax.rsqrt(jnp.mean(x * x, axis=-1, keepdims=True) + EPS) * g_ref[...]
        h_ref[...] = y.astype(BF16)

    o_ref[...] = lax.dot_general(h_ref[...], w_ref[...], (((1,), (1,)), ((), ())),
                                 preferred_element_type=F32)


def _norm_matmul(x, g, w_t, tm, tn):
    m, d = x.shape
    n = w_t.shape[0]
    return pl.pallas_call(
        _norm_matmul_kernel,
        grid=(m // tm, n // tn),
        in_specs=[pl.BlockSpec((tm, d), lambda i, j: (i, 0)),
                  pl.BlockSpec((1, d), lambda i, j: (0, 0)),
                  pl.BlockSpec((tn, d), lambda i, j: (j, 0))],
        out_specs=pl.BlockSpec((tm, tn), lambda i, j: (i, j)),
        out_shape=jax.ShapeDtypeStruct((m, n), F32),
        scratch_shapes=[pltpu.VMEM((tm, d), BF16)],
        compiler_params=_cparams(("parallel", "arbitrary")),
        name="in_proj",
    )(x, g, w_t)


def _rwkv_kernel(pr_ref, shift_ref, s0_ref, mu_ref, w0_ref, w2_ref, a0_ref, a2_ref, g2_ref,
                 kk_ref, ka_ref, rk_ref, lg_ref, lb_ref,
                 o_ref, s_out_ref, shift_out_ref, state_ref, prev_ref, *, chunk, n_sub, n_steps):
    c = pl.program_id(1)
    C = chunk
    R = n_sub * C
    PW = LANES
    subs = range(n_sub)
    pairs = range(N_PAIRS)
    units = [(s, p) for s in subs for p in pairs]
    uheads = [(s, p, hh) for (s, p) in units for hh in range(2)]

    @pl.when(c == 0)
    def _():
        state_ref[...] = jnp.zeros(state_ref.shape, F32)
        for h in range(N_HEADS):
            lo = (h % 2) * HEAD_DIM
            state_ref[h // 2, lo:lo + HEAD_DIM, lo:lo + HEAD_DIM] = s0_ref[0, h]
        prev_ref[...] = shift_ref[0]

    pr = pr_ref[...]
    row = lax.broadcasted_iota(jnp.int32, (R, 1), 0)
    shifted = jnp.where(row == 0, prev_ref[...], pltpu.roll(pr, 1, 0))
    u = pr + (shifted - pr) * mu_ref[...]
    prev_ref[...] = pr[R - 1:R, :]

    r = u[:, 0:W_H]
    k = u[:, W_H:2 * W_H]
    v = u[:, 2 * W_H:3 * W_H]
    wl = u[:, 3 * W_H:3 * W_H + DECAY_LORA]
    al = u[:, 3 * W_H + DECAY_LORA:3 * W_H + DECAY_LORA + AAA_LORA]
    gl = u[:, 3 * W_H + DECAY_LORA + AAA_LORA:R_COLS]

    w_raw = w0_ref[...] + _mm(jnp.tanh(wl), w2_ref[...])
    logw = -jnp.exp(_log_sigmoid(w_raw) - 0.5)
    a_sig = _sigmoid(a0_ref[...] + _mm(al, a2_ref[...]))
    gate = _mm(_sigmoid(gl), g2_ref[...])
    kk_all = k * kk_ref[...]
    k_new = k * (1.0 + (a_sig - 1.0) * ka_ref[...])

    ti = lax.broadcasted_iota(jnp.int32, (R, R), 0)
    si = lax.broadcasted_iota(jnp.int32, (R, R), 1)
    cum_mask = (si <= ti) & ((si // C) == (ti // C))
    g_cum = _mm_exact_rhs(cum_mask.astype(BF16), logw)
    e_incl = jnp.exp(g_cum)
    e_excl = jnp.exp(g_cum - logw)
    e_inv = jnp.exp(-g_cum)

    tci = lax.broadcasted_iota(jnp.int32, (C, C), 0)
    sci = lax.broadcasted_iota(jnp.int32, (C, C), 1)
    lower_incl = sci <= tci
    lower_strict = sci < tci

    bi = lax.broadcasted_iota(jnp.int32, (PW, PW), 0) // HEAD_DIM
    bj = lax.broadcasted_iota(jnp.int32, (PW, PW), 1) // HEAD_DIM
    blockdiag = bi == bj
    ones_bd = blockdiag.astype(BF16)

    def unit_cols(x):
        return {(s, p): x[s * C:(s + 1) * C, p * PW:(p + 1) * PW] for (s, p) in units}

    def head_sums(xs):
        x = jnp.concatenate([xs[un] for un in units], axis=0)
        s = jnp.dot(x.astype(BF16), ones_bd, preferred_element_type=F32)
        return {un: s[i * C:(i + 1) * C] for i, un in enumerate(units)}

    lane = lax.broadcasted_iota(jnp.int32, (1, PW), 1)
    in_h0 = lane < HEAD_DIM
    lane2 = lax.broadcasted_iota(jnp.int32, (1, 2 * PW), 1)
    in_h0_2 = (lane2 % PW) < HEAD_DIM

    r_u, k_u, v_u = unit_cols(r), unit_cols(k_new), unit_cols(v)
    kk_u = unit_cols(kk_all)
    asig_u = unit_cols(a_sig)
    ei_u, ee_u, ev_u = unit_cols(e_incl), unit_cols(e_excl), unit_cols(e_inv)
    pc_u = {(s, p): e_incl[(s + 1) * C - 1:(s + 1) * C, p * PW:(p + 1) * PW] for (s, p) in units}

    ss = head_sums({un: kk_u[un] * kk_u[un] for un in units})
    kk_u = {un: kk_u[un] * lax.rsqrt(ss[un] + 1e-12) for un in units}
    a_t = {un: -kk_u[un] * ee_u[un] for un in units}
    r_t = {un: r_u[un] * ei_u[un] for un in units}
    b_hat = {un: kk_u[un] * asig_u[un] * ev_u[un] for un in units}
    k_hat = {un: k_u[un] * ev_u[un] for un in units}
    b_til = {un: b_hat[un] * pc_u[un] for un in units}
    k_til = {un: k_hat[un] * pc_u[un] for un in units}

    left = {un: jnp.concatenate([a_t[un], r_t[un]], axis=0) for un in units}
    right = {un: jnp.concatenate([b_hat[un], k_hat[un]], axis=0) for un in units}
    tci2 = lax.broadcasted_iota(jnp.int32, (C, 2 * C), 0)
    sci2 = lax.broadcasted_iota(jnp.int32, (C, 2 * C), 1) % C
    strict2 = sci2 < tci2
    incl2 = sci2 <= tci2
    n_pow, a_rb, top, bot = {}, {}, {}, {}
    for (s, p, hh) in uheads:
        lm = jnp.where(in_h0 if hh == 0 else ~in_h0, left[s, p], 0.0)
        quad = _mm_nt(lm, right[s, p])
        top[s, p, hh] = jnp.where(strict2, quad[:C], 0.0)
        bot[s, p, hh] = jnp.where(incl2, quad[C:], 0.0)
        n_pow[s, p, hh] = top[s, p, hh][:, :C]
        a_rb[s, p, hh] = bot[s, p, hh][:, :C]

    def by_head(mats, un, x, mask):
        return jnp.where(mask, _mm(mats[un + (0,)], x), _mm(mats[un + (1,)], x))

    zv = {un: jnp.concatenate([jnp.zeros((C, PW), F32), v_u[un]], axis=0) for un in units}
    xs = {un: jnp.concatenate([a_t[un], by_head(top, un, zv[un], in_h0)], axis=1)
          for un in units}
    y_v = {un: by_head(bot, un, zv[un], in_h0) for un in units}
    n_levels = C.bit_length() - 1
    for lvl in range(n_levels):
        xs = {un: xs[un] + by_head(n_pow, un, xs[un], in_h0_2) for un in units}
        if lvl + 1 < n_levels:
            n_pow = {hd: _mm(n_pow[hd], n_pow[hd]) for hd in uheads}

    chained = n_sub > 1
    bk_til = {un: jnp.concatenate([b_til[un], k_til[un]], axis=0) for un in units}
    if chained:
        tr_m = {un: jnp.where(blockdiag, _mm_tn(xs[un][:, :PW], b_til[un]), 0.0) for un in units}
        tr_d = {un: jnp.where(blockdiag,
                              _mm_tn(jnp.concatenate([xs[un][:, PW:], v_u[un]], axis=0),
                                     bk_til[un]), 0.0) for un in units}

    st = [state_ref[p] for p in pairs]
    y_u = {}
    pending = None
    for s in list(subs) + [None]:
        if s is not None:
            if chained:
                chain = [_mm(st[p], tr_m[s, p]) + tr_d[s, p] for p in pairs]
            u_p = [_mm_nt(xs[s, p][:, :PW], st[p]) + xs[s, p][:, PW:] for p in pairs]
            rs_p = [_mm_nt(r_t[s, p], st[p]) for p in pairs]
            if not chained:
                chain = [jnp.where(blockdiag,
                                   _mm_tn(jnp.concatenate([u_p[p], v_u[s, p]], axis=0),
                                          bk_til[s, p]), 0.0) for p in pairs]
        if pending is not None:
            s0, u0, rs0 = pending
            for p in pairs:
                y_u[s0, p] = rs0[p] + by_head(a_rb, (s0, p), u0[p], in_h0) + y_v[s0, p]
        if s is not None:
            st = [st[p] * pc_u[s, p] + chain[p] for p in pairs]
            pending = (s, u_p, rs_p)
    for p in pairs:
        state_ref[p] = st[p]

    inv_n = 1.0 / HEAD_DIM
    mu_y = head_sums(y_u)
    yc = {un: y_u[un] - mu_y[un] * inv_n for un in units}
    var = head_sums({un: yc[un] * yc[un] for un in units})
    rk_row = rk_ref[...]
    bon = head_sums({(s, p): r_u[s, p] * k_u[s, p] * rk_row[:, p * PW:(p + 1) * PW]
                     for (s, p) in units})
    for (s, p) in units:
        cols = slice(p * PW, (p + 1) * PW)
        rows = slice(s * C, (s + 1) * C)
        yn = yc[s, p] * lax.rsqrt(var[s, p] * inv_n + GN_EPS) * lg_ref[:, cols] + lb_ref[:, cols]
        o_ref[rows, cols] = ((yn + bon[s, p] * v_u[s, p]) * gate[rows, cols]).astype(o_ref.dtype)

    @pl.when(c == n_steps - 1)
    def _():
        for h in range(N_HEADS):
            lo = (h % 2) * HEAD_DIM
            s_out_ref[0, h] = state_ref[h // 2, lo:lo + HEAD_DIM, lo:lo + HEAD_DIM]
        shift_out_ref[0] = pr[R - 1:R, :]


def _rwkv(p_all, shift_in, s0, lw, *, row0, n_seq, seq_len, chunk, n_sub):
    rows = chunk * n_sub
    n_steps = seq_len // rows
    rb0 = row0 // rows
    vec = lambda n: pl.BlockSpec((1, n), lambda b, c: (0, 0))
    mat = lambda k, n: pl.BlockSpec((k, n), lambda b, c: (0, 0))
    kern = functools.partial(_rwkv_kernel, chunk=chunk, n_sub=n_sub, n_steps=n_steps)
    state_spec = pl.BlockSpec((1, N_HEADS, HEAD_DIM, HEAD_DIM), lambda b, c: (b, 0, 0, 0))
    shift_spec = pl.BlockSpec((1, 1, R_COLS), lambda b, c: (b, 0, 0))
    return pl.pallas_call(
        kern,
        grid=(n_seq, n_steps),
        in_specs=[
            pl.BlockSpec((rows, R_COLS), lambda b, c: (rb0 + b * n_steps + c, C32_R // R_COLS)),
            shift_spec, state_spec,
            vec(R_COLS), vec(W_H), mat(DECAY_LORA, W_H), vec(W_H), mat(AAA_LORA, W_H),
            mat(GATE_LORA, W_H), vec(W_H), vec(W_H), vec(W_H), vec(W_H), vec(W_H)],
        out_specs=[pl.BlockSpec((rows, W_H), lambda b, c: (b * n_steps + c, 0)),
                   state_spec, shift_spec],
        out_shape=[jax.ShapeDtypeStruct((n_seq * seq_len, W_H), BF16),
                   jax.ShapeDtypeStruct((n_seq, N_HEADS, HEAD_DIM, HEAD_DIM), F32),
                   jax.ShapeDtypeStruct((n_seq, 1, R_COLS), F32)],
        scratch_shapes=[pltpu.VMEM((N_PAIRS, LANES, LANES), F32),
                        pltpu.VMEM((1, R_COLS), F32)],
        compiler_params=_cparams(("parallel", "arbitrary")),
        name=f"rwkv_c{chunk}",
    )(p_all, shift_in, s0, lw['mu'], lw['w0'], lw['w2'], lw['a0'], lw['a2'], lw['g2'],
      lw['k_k'], lw['k_a'], lw['r_k'], lw['lnx_g'], lw['lnx_b'])


def _place_matrices():
    pk = np.zeros((W_H, N_HEADS * LANES), np.float32)
    pc = np.zeros((3 * LANES, N_HEADS * LANES), np.float32)
    ones = np.zeros((1, N_HEADS * LANES), np.float32)
    for h in range(N_HEADS):
        for j in range(HEAD_DIM):
            pk[h * HEAD_DIM + j, h * LANES + j] = 1.0
        for piece in range(3):
            pc[piece * LANES + h, h * LANES + CK_LANE0 + piece] = 1.0
            ones[0, h * LANES + CQ_LANE0 + piece] = 1.0
    return jnp.asarray(pk, BF16), jnp.asarray(pc, BF16), jnp.asarray(ones, F32)


def _fox_prep_kernel(k_ref, v_ref, fl_ref, bf_ref, pk_ref, pc_ref, ones_ref, kprev_ref, vprev_ref,
                     kall_ref, vall_ref, logft_ref, c_ref, kaug_ref, vt_ref, carry_ref):
    del kprev_ref, vprev_ref

    @pl.when(pl.program_id(1) == 0)
    def _():
        carry_ref[...] = jnp.zeros_like(carry_ref)

    tc = fl_ref.shape[0]
    k = k_ref[...]
    v_t = v_ref[...].T
    kall_ref[0, 0] = k.T
    vall_ref[0, 0] = v_t
    ones_half = jnp.ones((HEAD_DIM, tc), F32)
    for h in range(N_HEADS):
        vh = v_t[h * HEAD_DIM:(h + 1) * HEAD_DIM]
        grp = [vh, ones_half] if h % 2 == 0 else [ones_half, vh]
        vt_ref[0, h * LANES:(h + 1) * LANES, :] = jnp.concatenate(grp, axis=0).astype(BF16)
    logf = _log_sigmoid(fl_ref[...] + bf_ref[...])
    logft_ref[0] = logf.T[0:N_HEADS, :]
    ti = lax.broadcasted_iota(jnp.int32, (tc, tc), 0)
    si = lax.broadcasted_iota(jnp.int32, (tc, tc), 1)
    c = _mm_exact_rhs((si <= ti).astype(BF16), logf) + carry_ref[...]
    carry_ref[...] = c[tc - 1:tc, :]
    c2 = c * LOG2E
    c_ref[...] = c2
    pieces = jnp.concatenate(_split3(-c2), axis=1)
    kaug = (jnp.dot(k.astype(BF16), pk_ref[...], preferred_element_type=F32)
            + jnp.dot(pieces, pc_ref[...], preferred_element_type=F32) + ones_ref[...])
    kaug_ref[...] = kaug.astype(BF16)


def _fox_prep(p_all, bf_pad, places, k_prev, v_prev, layer, *, n_seq, seq_len):
    tc = _tile(seq_len, (PREP_BLOCK, 256, 128))
    nb = seq_len // tc
    rows = n_seq * seq_len
    const = lambda a: pl.BlockSpec(a.shape, lambda b, j: (0, 0))
    rowblk = lambda n: pl.BlockSpec((tc, n), lambda b, j: (b * nb + j, 0))
    stacked = pl.BlockSpec((1, 1, W_H, tc), lambda b, j: (layer, b, 0, j))
    return pl.pallas_call(
        _fox_prep_kernel,
        grid=(n_seq, nb),
        in_specs=[pl.BlockSpec((tc, W_H), lambda b, j: (b * nb + j, C32_K // W_H)),
                  pl.BlockSpec((tc, W_H), lambda b, j: (b * nb + j, C32_V // W_H)),
                  pl.BlockSpec((tc, LANES), lambda b, j: (b * nb + j, C32_FL // LANES)),
                  pl.BlockSpec((1, LANES), lambda b, j: (0, 0)),
                  const(places[0]), const(places[1]), const(places[2]),
                  pl.BlockSpec(memory_space=pl.ANY), pl.BlockSpec(memory_space=pl.ANY)],
        out_specs=[stacked, stacked,
                   pl.BlockSpec((1, N_HEADS, tc), lambda b, j: (b, 0, j)),
                   rowblk(LANES), rowblk(N_HEADS * LANES),
                   pl.BlockSpec((1, N_HEADS * LANES, tc), lambda b, j: (b, 0, j))],
        out_shape=[jax.ShapeDtypeStruct(k_prev.shape, F32),
                   jax.ShapeDtypeStruct(v_prev.shape, F32),
                   jax.ShapeDtypeStruct((n_seq, N_HEADS, seq_len), F32),
                   jax.ShapeDtypeStruct((rows, LANES), F32),
                   jax.ShapeDtypeStruct((rows, N_HEADS * LANES), BF16),
                   jax.ShapeDtypeStruct((n_seq, N_HEADS * LANES, seq_len), BF16)],
        scratch_shapes=[pltpu.VMEM((1, LANES), F32)],
        input_output_aliases={7: 0, 8: 1},
        compiler_params=_cparams(("parallel", "arbitrary")),
        name="fox_prep",
    )(p_all, p_all, p_all, bf_pad, *places, k_prev, v_prev)


def _fox_prompt_kernel(q_ref, c_ref, kaug_ref, vt_ref, o_ref, m_ref, acc_ref, sa_ref, sb_ref,
                       *, blk):
    hp = pl.program_id(1)
    i = pl.program_id(2)
    lane = lax.broadcasted_iota(jnp.int32, (1, LANES), 1)
    c_blk = c_ref[...]
    q_all = q_ref[...].astype(F32) * (SCALE * LOG2E)
    q_aug = []
    for hh in range(2):
        q_h = q_all if hh == 0 else pltpu.roll(q_all, HEAD_DIM, 1)
        cq = jnp.sum(jnp.where(lane == 2 * hp + hh, c_blk, 0.0), axis=-1, keepdims=True)
        hi, mid, lo = _split3(cq)
        aug = jnp.where(lane < HEAD_DIM, q_h,
                        jnp.where(lane == CQ_LANE0, hi.astype(F32),
                                  jnp.where(lane == CQ_LANE0 + 1, mid.astype(F32),
                                            jnp.where(lane == CQ_LANE0 + 2, lo.astype(F32),
                                                      jnp.where(lane < CK_LANE0 + 3, 1.0, 0.0)))))
        q_aug.append(aug.astype(BF16))

    m_ref[...] = jnp.full(m_ref.shape, -jnp.inf, F32)
    acc_ref[...] = jnp.zeros(acc_ref.shape, F32)
    key_i = lax.broadcasted_iota(jnp.int32, (blk, blk), 0)
    qry_i = lax.broadcasted_iota(jnp.int32, (blk, blk), 1)
    causal = key_i <= qry_i

    def scores(j, dst_ref):
        start = pl.multiple_of(j * blk, blk)
        for hh in range(2):
            dst_ref[hh] = lax.dot_general(
                kaug_ref[pl.ds(start, blk), hh * LANES:(hh + 1) * LANES], q_aug[hh],
                (((1,), (1,)), ((), ())), preferred_element_type=F32)

    def softmax_update(j, src_ref, masked):
        start = pl.multiple_of(j * blk, blk)
        for hh in range(2):
            vt = vt_ref[0, hh * LANES:(hh + 1) * LANES, pl.ds(start, blk)]
            st = src_ref[hh]
            if masked:
                st = jnp.where(causal, st, -jnp.inf)
            m_old = m_ref[hh]
            m_new = jnp.maximum(m_old, jnp.max(st, axis=0, keepdims=True))
            alpha = jnp.exp2(m_old - m_new)
            pt = jnp.exp2(st - m_new).astype(BF16)
            m_ref[hh] = m_new
            acc_ref[hh] = alpha * acc_ref[hh] + jnp.dot(vt, pt, preferred_element_type=F32)

    scores(0, sa_ref)

    def body(t, carry):
        j = 2 * t
        scores(j + 1, sb_ref)
        softmax_update(j, sa_ref, False)
        scores(j + 2, sa_ref)
        softmax_update(j + 1, sb_ref, False)
        return carry

    lax.fori_loop(0, i // 2, body, 0)

    @pl.when(i % 2 == 0)
    def _():
        softmax_update(i, sa_ref, True)

    @pl.when(i % 2 == 1)
    def _():
        scores(i, sb_ref)
        softmax_update(i - 1, sa_ref, False)
        softmax_update(i, sb_ref, True)

    chan = lax.broadcasted_iota(jnp.int32, (LANES, 1), 0)
    acc0 = acc_ref[0]
    acc1 = acc_ref[1]
    out_t = jnp.where(chan < HEAD_DIM, acc0 / acc0[HEAD_DIM:HEAD_DIM + 1, :], acc1 / acc1[0:1, :])
    o_ref[...] = out_t.T.astype(o_ref.dtype)


def _fox_prompt(p16, c, kaug, v_t, *, n_seq, seq_len):
    blk = _tile(seq_len, (ATT_BLOCK, 256, 128))
    nq = seq_len // blk
    kern = functools.partial(_fox_prompt_kernel, blk=blk)
    return pl.pallas_call(
        kern,
        grid=(n_seq, N_PAIRS, nq),
        in_specs=[pl.BlockSpec((blk, LANES), lambda b, hp, i: (b * nq + i, C16_Q // LANES + hp)),
                  pl.BlockSpec((blk, LANES), lambda b, hp, i: (b * nq + i, 0)),
                  pl.BlockSpec((seq_len, KAUG_W), lambda b, hp, i: (b, hp)),
                  pl.BlockSpec((1, 2 * LANES, seq_len), lambda b, hp, i: (b, hp, 0))],
        out_specs=pl.BlockSpec((blk, LANES), lambda b, hp, i: (b * nq + i, hp)),
        out_shape=jax.ShapeDtypeStruct((n_seq * seq_len, W_H), BF16),
        scratch_shapes=[pltpu.VMEM((2, 1, blk), F32), pltpu.VMEM((2, LANES, blk), F32),
                        pltpu.VMEM((2, blk, blk), F32), pltpu.VMEM((2, blk, blk), F32)],
        compiler_params=_cparams(("parallel", "parallel", "arbitrary")),
        name="fox_prompt",
    )(p16, c, kaug, v_t)


def _fox_sample_kernel(q_ref, k_ref, v_ref, fl_ref, bf_ref, ckt_ref, cvt_ref, clt_ref,
                       kprev_ref, vprev_ref, o_ref, logf_ref, kall_ref, vall_ref, *, t_new, past):
    del kprev_ref, vprev_ref
    T = t_new
    kall_ref[0] = k_ref[...]
    vall_ref[0] = v_ref[...]
    logf = _log_sigmoid(fl_ref[...] + bf_ref[...])
    logf_ref[...] = logf
    logf_pad = jnp.concatenate([logf, jnp.zeros((LANES - T, LANES), F32)], axis=0)
    ti = lax.broadcasted_iota(jnp.int32, (LANES, LANES), 0)
    si = lax.broadcasted_iota(jnp.int32, (LANES, LANES), 1)
    cnew_pad = _mm_exact_rhs((si <= ti).astype(BF16), logf_pad)
    cnew_t = cnew_pad.T
    cnew = cnew_pad[0:T, :]
    suf = clt_ref[0, 0]
    pos = lax.broadcasted_iota(jnp.int32, suf.shape, 1)
    total = suf
    d = 1
    while d < past:
        total = total + jnp.where(pos + d < past, pltpu.roll(total, past - d, 1), 0.0)
        d *= 2
    suf = total - suf

    lane = lax.broadcasted_iota(jnp.int32, (T, LANES), 1)
    rowi = lax.broadcasted_iota(jnp.int32, (T, T), 0)
    coli = lax.broadcasted_iota(jnp.int32, (T, T), 1)
    for h in range(N_HEADS):
        sl = slice(h * HEAD_DIM, (h + 1) * HEAD_DIM)
        q = q_ref[:, sl].astype(F32) * SCALE
        cq = jnp.sum(jnp.where(lane == h, cnew, 0.0), axis=-1, keepdims=True)
        s_c = _mm(q, ckt_ref[0, 0, h]) + (cq + suf[h:h + 1, :])
        s_n = _mm_nt(q, k_ref[:, sl]) + (cq - cnew_t[h:h + 1, 0:T])
        s_n = jnp.where(coli <= rowi, s_n, -jnp.inf)
        m = jnp.maximum(jnp.max(s_c, axis=-1, keepdims=True), jnp.max(s_n, axis=-1, keepdims=True))
        p_c = jnp.exp(s_c - m)
        p_n = jnp.exp(s_n - m)
        l = jnp.sum(p_c, axis=-1, keepdims=True) + jnp.sum(p_n, axis=-1, keepdims=True)
        o = _mm_nt(p_c, cvt_ref[0, 0, h]) + _mm(p_n, v_ref[:, sl])
        o_ref[:, sl] = (o / l).astype(o_ref.dtype)


def _fox_sample(p16, p32, bf_pad, cache_kt, cache_vt, cache_logf_t, k_prev, v_prev, layer,
                *, row0, n_seq, t_new):
    past = cache_kt.shape[4]
    rb0 = row0 // t_new
    blk512 = lambda col: pl.BlockSpec((t_new, W_H), lambda b: (rb0 + b, col // W_H))
    cache_spec = pl.BlockSpec((1, 1, N_HEADS, HEAD_DIM, past), lambda b: (layer, b, 0, 0, 0))
    stacked = pl.BlockSpec((1, t_new, W_H), lambda b: (layer, b, 0))
    return pl.pallas_call(
        functools.partial(_fox_sample_kernel, t_new=t_new, past=past),
        grid=(n_seq,),
        in_specs=[blk512(C16_Q), blk512(C32_K), blk512(C32_V),
                  pl.BlockSpec((t_new, LANES), lambda b: (rb0 + b, C32_FL // LANES)),
                  pl.BlockSpec((1, LANES), lambda b: (0, 0)),
                  cache_spec, cache_spec,
                  pl.BlockSpec((1, 1, N_HEADS, past), lambda b: (layer, b, 0, 0)),
                  pl.BlockSpec(memory_space=pl.ANY), pl.BlockSpec(memory_space=pl.ANY)],
        out_specs=[pl.BlockSpec((t_new, W_H), lambda b: (b, 0)),
                   pl.BlockSpec((t_new, LANES), lambda b: (b, 0)), stacked, stacked],
        out_shape=[jax.ShapeDtypeStruct((n_seq * t_new, W_H), BF16),
                   jax.ShapeDtypeStruct((n_seq * t_new, LANES), F32),
                   jax.ShapeDtypeStruct(k_prev.shape, F32),
                   jax.ShapeDtypeStruct(v_prev.shape, F32)],
        input_output_aliases={8: 2, 9: 3},
        compiler_params=_cparams(("parallel",)),
        name="fox_sample",
    )(p16, p32, p32, p32, bf_pad, cache_kt, cache_vt, cache_logf_t, k_prev, v_prev)


def _merge_kernel(x_ref, oap_ref, oas_ref, obp_ref, obs_ref, ga_ref, gb_ref, pa_ref, pb_ref, wo_ref,
                  o_ref, *, n_prompt_tiles):
    is_sample = pl.program_id(0) >= n_prompt_tiles
    oa = jnp.where(is_sample, oas_ref[...], oap_ref[...])
    ob = jnp.where(is_sample, obs_ref[...], obp_ref[...])
    ma = jnp.dot(oa, pa_ref[...], preferred_element_type=F32)
    mb = jnp.dot(ob, pb_ref[...], preferred_element_type=F32)
    mix = _sigmoid(ga_ref[...].astype(F32)) * ma + _sigmoid(gb_ref[...].astype(F32)) * mb
    o_ref[...] = x_ref[...] + jnp.dot(mix.astype(BF16), wo_ref[...], preferred_element_type=F32)


def _merge(x, oa_p, oa_s, ob_p, ob_s, p16, pa, pb, wo, tm):
    m, d = x.shape
    npt = oa_p.shape[0] // tm
    row = lambda n, col=0: pl.BlockSpec((tm, n), lambda i: (i, col))
    prompt = pl.BlockSpec((tm, W_H), lambda i: (jnp.minimum(i, npt - 1), 0))
    sample = pl.BlockSpec((tm, W_H), lambda i: (jnp.maximum(i - npt, 0), 0))
    full = lambda a: pl.BlockSpec(a.shape, lambda i: (0, 0))
    return pl.pallas_call(
        functools.partial(_merge_kernel, n_prompt_tiles=npt),
        grid=(m // tm,),
        in_specs=[row(d), prompt, sample, prompt, sample, row(d, C16_GA // d), row(d, C16_GB // d),
                  full(pa), full(pb), full(wo)],
        out_specs=row(d),
        out_shape=jax.ShapeDtypeStruct((m, d), F32),
        compiler_params=_cparams(("parallel",)),
        name="merge",
    )(x, oa_p, oa_s, ob_p, ob_s, p16, p16, pa, pb, wo)


def _ffn_kernel(x_ref, g_ref, wg_ref, wu_ref, wd_ref, fg_ref, o_ref, h_ref, acc_ref, *, final):
    f = pl.program_id(1)

    @pl.when(f == 0)
    def _():
        x = x_ref[...]
        y = x * lax.rsqrt(jnp.mean(x * x, axis=-1, keepdims=True) + EPS) * g_ref[...]
        h_ref[...] = y.astype(BF16)
        acc_ref[...] = x

    h = h_ref[...]
    gate = jnp.dot(h, wg_ref[...], preferred_element_type=F32)
    up = jnp.dot(h, wu_ref[...], preferred_element_type=F32)
    act = gate * _sigmoid(gate) * up
    acc_ref[...] += jnp.dot(act.astype(BF16), wd_ref[...], preferred_element_type=F32)

    @pl.when(f == pl.num_programs(1) - 1)
    def _():
        y = acc_ref[...]
        if final:
            y = y * lax.rsqrt(jnp.mean(y * y, axis=-1, keepdims=True) + EPS) * fg_ref[...]
        o_ref[...] = y


def _ffn(x, g, wg, wu, wd, fg, tm, tf, final):
    m, d = x.shape
    dff = wg.shape[1]
    return pl.pallas_call(
        functools.partial(_ffn_kernel, final=final),
        grid=(m // tm, dff // tf),
        in_specs=[pl.BlockSpec((tm, d), lambda i, f: (i, 0)),
                  pl.BlockSpec((1, d), lambda i, f: (0, 0)),
                  pl.BlockSpec((d, tf), lambda i, f: (0, f)),
                  pl.BlockSpec((d, tf), lambda i, f: (0, f)),
                  pl.BlockSpec((tf, d), lambda i, f: (f, 0)),
                  pl.BlockSpec((1, d), lambda i, f: (0, 0))],
        out_specs=pl.BlockSpec((tm, d), lambda i, f: (i, 0)),
        out_shape=jax.ShapeDtypeStruct((m, d), F32),
        scratch_shapes=[pltpu.VMEM((tm, d), BF16), pltpu.VMEM((tm, d), F32)],
        compiler_params=_cparams(("parallel", "arbitrary")),
        name="ffn",
    )(x, g, wg, wu, wd, fg)


def kernel(x_prompt, x_sample, cache_fox_k, cache_fox_v, cache_fox_logf, state_rwkv, state_shift,
           norm1_g, w_in, rwkv_mu, rwkv_w0, rwkv_w2, rwkv_a0, rwkv_a2, rwkv_g2, rwkv_k_k, rwkv_k_a,
           rwkv_r_k, rwkv_lnx_g, rwkv_lnx_b, fox_bf, p_a, p_b, w_out, norm2_g, w_gate, w_up, w_down,
           final_g):
    n_b, seq, d = x_prompt.shape
    n_db, t_new, _ = x_sample.shape
    depth = w_in.shape[0]
    d_ff = w_gate.shape[2]
    m_prompt = n_b * seq
    m_sample = n_db * t_new
    m_all = m_prompt + m_sample

    o_q = R_COLS
    o_fl = R_COLS + 3 * W_H
    o_g = o_fl + N_HEADS
    w_t = jnp.swapaxes(w_in, 1, 2)
    w_cat_t = jnp.concatenate(
        [w_t[:, o_g:o_g + 2 * d], w_t[:, o_q:o_q + 3 * W_H], w_t[:, 0:R_COLS],
         w_t[:, o_fl:o_fl + N_HEADS],
         jnp.zeros((depth, P_COLS - C32_FL - N_HEADS, d), w_in.dtype)], axis=1).astype(BF16)
    bf_pad = jnp.pad(fox_bf, ((0, 0), (0, LANES - N_HEADS)))
    pa_b, pb_b, wo_b = p_a.astype(BF16), p_b.astype(BF16), w_out.astype(BF16)
    wg_b, wu_b, wd_b = w_gate.astype(BF16), w_up.astype(BF16), w_down.astype(BF16)
    w2_b, a2_b, g2_b = rwkv_w2.astype(BF16), rwkv_a2.astype(BF16), rwkv_g2.astype(BF16)
    cache_kt = jnp.transpose(cache_fox_k, (0, 1, 3, 4, 2))
    cache_vt = jnp.transpose(cache_fox_v, (0, 1, 3, 4, 2))
    cache_logf_t = jnp.swapaxes(cache_fox_logf, 2, 3)
    places = _place_matrices()

    tm = _tile(math.gcd(m_prompt, m_sample))
    tm_w = _tile(m_all, (1536, 512, 256, 128, 64, 32, 16, 8))
    tn = _tile(P_COLS, (1408, 512))
    tm_f = _tile(m_all, (768, 512, 256, 128, 64, 32, 16, 8))
    tf = _tile(d_ff, (1408, 256, 128))
    x = jnp.concatenate([x_prompt.reshape(m_prompt, d), x_sample.reshape(m_sample, d)], axis=0)
    zero_shift = jnp.zeros((n_b, 1, R_COLS), F32)
    zero_state = jnp.zeros((n_b, N_HEADS, HEAD_DIM, HEAD_DIM), F32)
    n_sub = PROMPT_SUBCHUNKS if seq % (PROMPT_CHUNK * PROMPT_SUBCHUNKS) == 0 else 1

    outs = {k: [] for k in ('lfp', 'sp', 'shp', 'lfd', 'sd', 'shd')}
    kp = jnp.zeros((depth, n_b, W_H, seq), F32)
    vp = jnp.zeros((depth, n_b, W_H, seq), F32)
    kd = jnp.zeros((depth, m_sample, W_H), F32)
    vd = jnp.zeros((depth, m_sample, W_H), F32)
    for l in range(depth):
        row = lambda a: a[l].reshape(1, -1)
        lw = dict(mu=row(rwkv_mu), w0=row(rwkv_w0), w2=w2_b[l], a0=row(rwkv_a0), a2=a2_b[l],
                  g2=g2_b[l], k_k=row(rwkv_k_k), k_a=row(rwkv_k_a), r_k=row(rwkv_r_k),
                  lnx_g=row(rwkv_lnx_g), lnx_b=row(rwkv_lnx_b))
        p16 = p32 = _norm_matmul(x, row(norm1_g), w_cat_t[l], tm_w, tn)

        oa_p, s_p, sh_p = _rwkv(p32, zero_shift, zero_state, lw, row0=0, n_seq=n_b,
                                seq_len=seq, chunk=PROMPT_CHUNK, n_sub=n_sub)
        oa_s, s_d, sh_d = _rwkv(p32, state_shift[l], state_rwkv[l], lw, row0=m_prompt,
                                n_seq=n_db, seq_len=t_new, chunk=t_new, n_sub=1)

        kp, vp, logft_p, c_p, kaug, v_t = _fox_prep(p32, bf_pad[l:l + 1], places, kp, vp, l,
                                                    n_seq=n_b, seq_len=seq)
        ob_p = _fox_prompt(p16, c_p, kaug, v_t, n_seq=n_b, seq_len=seq)
        ob_s, logf_d, kd, vd = _fox_sample(p16, p32, bf_pad[l:l + 1], cache_kt, cache_vt,
                                           cache_logf_t, kd, vd, l, row0=m_prompt, n_seq=n_db,
                                           t_new=t_new)

        x = _merge(x, oa_p, oa_s, ob_p, ob_s, p16, pa_b[l], pb_b[l], wo_b[l], tm)
        x = _ffn(x, row(norm2_g), wg_b[l], wu_b[l], wd_b[l], final_g.reshape(1, -1), tm_f, tf,
                 final=(l == depth - 1))

        outs['lfp'].append(logft_p)
        outs['lfd'].append(logf_d[:, :N_HEADS].reshape(n_db, t_new, N_HEADS))
        outs['sp'].append(s_p)
        outs['shp'].append(sh_p)
        outs['sd'].append(s_d)
        outs['shd'].append(sh_d)

    st = {k: jnp.stack(v) for k, v in outs.items()}
    prompt_heads = lambda a: jnp.transpose(a.reshape(depth, n_b, N_HEADS, HEAD_DIM, seq),
                                           (0, 1, 4, 2, 3))
    sample_heads = lambda a: a.reshape(depth, n_db, t_new, N_HEADS, HEAD_DIM)
    y_prompt = x[:m_prompt].reshape(n_b, seq, d)
    y_sample = x[m_prompt:].reshape(n_db, t_new, d)
    return (y_prompt, y_sample, prompt_heads(kp), prompt_heads(vp), jnp.swapaxes(st['lfp'], 2, 3),
            st['sp'], st['shp'], sample_heads(kd), sample_heads(vd), st['lfd'], st['sd'],
            st['shd'])
```

```python
import functools
import math

import numpy as np
import jax
import jax.numpy as jnp
from jax import lax
from jax.experimental import pallas as pl
from jax.experimental.pallas import tpu as pltpu

F32 = jnp.float32
BF16 = jnp.bfloat16

HEAD_DIM = 64
N_HEADS = 8
N_PAIRS = N_HEADS // 2
W_H = N_HEADS * HEAD_DIM
DECAY_LORA = 64
AAA_LORA = 64
GATE_LORA = 128
R_COLS = 3 * W_H + DECAY_LORA + AAA_LORA + GATE_LORA
EPS = 1e-6
GN_EPS = 64e-5
SCALE = HEAD_DIM ** -0.5
LOG2E = 1.4426950408889634

LANES = 128
VMEM_LIMIT_BYTES = 56 * 1024 * 1024

C16_GA = 0
C16_GB = 1024
C16_Q = 2048
C32_K = 2560
C32_V = 3072
C32_R = 3584
C32_FL = 5376
P_COLS = 5632
PROMPT_CHUNK = 64
PROMPT_SUBCHUNKS = 4
ATT_BLOCK = 512
PREP_BLOCK = 512
KAUG_W = 2 * LANES
CQ_LANE0 = HEAD_DIM
CK_LANE0 = HEAD_DIM + 3


def _cparams(sem):
    return pltpu.CompilerParams(dimension_semantics=sem, vmem_limit_bytes=VMEM_LIMIT_BYTES)


def _mm(a, b):
    return jnp.dot(a.astype(BF16), b.astype(BF16), preferred_element_type=F32)


def _mm_nt(a, b):
    return lax.dot_general(a.astype(BF16), b.astype(BF16), (((1,), (1,)), ((), ())),
                           preferred_element_type=F32)


def _mm_tn(a, b):
    return lax.dot_general(a.astype(BF16), b.astype(BF16), (((0,), (0,)), ((), ())),
                           preferred_element_type=F32)


def _split3(x):
    hi = x.astype(BF16)
    r = x - hi.astype(F32)
    mid = r.astype(BF16)
    lo = (r - mid.astype(F32)).astype(BF16)
    return hi, mid, lo


def _mm_exact_rhs(m01, x):
    hi, mid, lo = _split3(x)
    d = lambda p: jnp.dot(m01, p, preferred_element_type=F32)
    return d(hi) + d(mid) + d(lo)


def _log_sigmoid(z):
    return jnp.minimum(z, 0.0) - jnp.log1p(jnp.exp(-jnp.abs(z)))


def _sigmoid(z):
    return 1.0 / (1.0 + jnp.exp(-z))


def _tile(m, cands=(512, 256, 128, 64, 32, 16, 8)):
    for t in cands:
        if m % t == 0:
            return t
    raise ValueError(m)


def _norm_matmul_kernel(x_ref, g_ref, w_ref, o_ref, h_ref):
    @pl.when(pl.program_id(1) == 0)
    def _():
        x = x_ref[...]
        y = x * lax.rsqrt(jnp.mean(x * x, axis=-1, keepdims=True) + EPS) * g_ref[...]
        h_ref[...] = y.astype(BF16)

    o_ref[...] = lax.dot_general(h_ref[...], w_ref[...], (((1,), (1,)), ((), ())),
                                 preferred_element_type=F32)


def _norm_matmul(x, g, w_t, tm, tn):
    m, d = x.shape
    n = w_t.shape[0]
    return pl.pallas_call(
        _norm_matmul_kernel,
        grid=(m // tm, n // tn),
        in_specs=[pl.BlockSpec((tm, d), lambda i, j: (i, 0)),
                  pl.BlockSpec((1, d), lambda i, j: (0, 0)),
                  pl.BlockSpec((tn, d), lambda i, j: (j, 0))],
        out_specs=pl.BlockSpec((tm, tn), lambda i, j: (i, j)),
        out_shape=jax.ShapeDtypeStruct((m, n), F32),
        scratch_shapes=[pltpu.VMEM((tm, d), BF16)],
        compiler_params=_cparams(("parallel", "arbitrary")),
        name="in_proj",
    )(x, g, w_t)


def _rwkv_kernel(pr_ref, shift_ref, s0_ref, mu_ref, w0_ref, w2_ref, a0_ref, a2_ref, g2_ref,
                 kk_ref, ka_ref, rk_ref, lg_ref, lb_ref,
                 o_ref, s_out_ref, shift_out_ref, state_ref, prev_ref, *, chunk, n_sub, n_steps):
    c = pl.program_id(1)
    C = chunk
    R = n_sub * C
    PW = LANES
    subs = range(n_sub)
    pairs = range(N_PAIRS)
    units = [(s, p) for s in subs for p in pairs]
    uheads = [(s, p, hh) for (s, p) in units for hh in range(2)]

    @pl.when(c == 0)
    def _():
        state_ref[...] = jnp.zeros(state_ref.shape, F32)
        for h in range(N_HEADS):
            lo = (h % 2) * HEAD_DIM
            state_ref[h // 2, lo:lo + HEAD_DIM, lo:lo + HEAD_DIM] = s0_ref[0, h]
        prev_ref[...] = shift_ref[0]

    pr = pr_ref[...]
    row = lax.broadcasted_iota(jnp.int32, (R, 1), 0)
    shifted = jnp.where(row == 0, prev_ref[...], pltpu.roll(pr, 1, 0))
    u = pr + (shifted - pr) * mu_ref[...]
    prev_ref[...] = pr[R - 1:R, :]

    r = u[:, 0:W_H]
    k = u[:, W_H:2 * W_H]
    v = u[:, 2 * W_H:3 * W_H]
    wl = u[:, 3 * W_H:3 * W_H + DECAY_LORA]
    al = u[:, 3 * W_H + DECAY_LORA:3 * W_H + DECAY_LORA + AAA_LORA]
    gl = u[:, 3 * W_H + DECAY_LORA + AAA_LORA:R_COLS]

    w_raw = w0_ref[...] + _mm(jnp.tanh(wl), w2_ref[...])
    logw = -jnp.exp(_log_sigmoid(w_raw) - 0.5)
    a_sig = _sigmoid(a0_ref[...] + _mm(al, a2_ref[...]))
    gate = _mm(_sigmoid(gl), g2_ref[...])
    kk_all = k * kk_ref[...]
    k_new = k * (1.0 + (a_sig - 1.0) * ka_ref[...])

    ti = lax.broadcasted_iota(jnp.int32, (R, R), 0)
    si = lax.broadcasted_iota(jnp.int32, (R, R), 1)
    cum_mask = (si <= ti) & ((si // C) == (ti // C))
    g_cum = _mm_exact_rhs(cum_mask.astype(BF16), logw)
    e_incl = jnp.exp(g_cum)
    e_excl = jnp.exp(g_cum - logw)
    e_inv = jnp.exp(-g_cum)

    tci = lax.broadcasted_iota(jnp.int32, (C, C), 0)
    sci = lax.broadcasted_iota(jnp.int32, (C, C), 1)
    lower_incl = sci <= tci
    lower_strict = sci < tci

    bi = lax.broadcasted_iota(jnp.int32, (PW, PW), 0) // HEAD_DIM
    bj = lax.broadcasted_iota(jnp.int32, (PW, PW), 1) // HEAD_DIM
    blockdiag = bi == bj
    ones_bd = blockdiag.astype(BF16)

    def unit_cols(x):
        return {(s, p): x[s * C:(s + 1) * C, p * PW:(p + 1) * PW] for (s, p) in units}

    def head_sums(xs):
        x = jnp.concatenate([xs[un] for un in units], axis=0)
        s = jnp.dot(x.astype(BF16), ones_bd, preferred_element_type=F32)
        return {un: s[i * C:(i + 1) * C] for i, un in enumerate(units)}

    lane = lax.broadcasted_iota(jnp.int32, (1, PW), 1)
    in_h0 = lane < HEAD_DIM
    lane2 = lax.broadcasted_iota(jnp.int32, (1, 2 * PW), 1)
    in_h0_2 = (lane2 % PW) < HEAD_DIM

    r_u, k_u, v_u = unit_cols(r), unit_cols(k_new), unit_cols(v)
    kk_u = unit_cols(kk_all)
    asig_u = unit_cols(a_sig)
    ei_u, ee_u, ev_u = unit_cols(e_incl), unit_cols(e_excl), unit_cols(e_inv)
    pc_u = {(s, p): e_incl[(s + 1) * C - 1:(s + 1) * C, p * PW:(p + 1) * PW] for (s, p) in units}

    ss = head_sums({un: kk_u[un] * kk_u[un] for un in units})
    kk_u = {un: kk_u[un] * lax.rsqrt(ss[un] + 1e-12) for un in units}
    a_t = {un: -kk_u[un] * ee_u[un] for un in units}
    r_t = {un: r_u[un] * ei_u[un] for un in units}
    b_hat = {un: kk_u[un] * asig_u[un] * ev_u[un] for un in units}
    k_hat = {un: k_u[un] * ev_u[un] for un in units}
    b_til = {un: b_hat[un] * pc_u[un] for un in units}
    k_til = {un: k_hat[un] * pc_u[un] for un in units}

    left = {un: jnp.concatenate([a_t[un], r_t[un]], axis=0) for un in units}
    right = {un: jnp.concatenate([b_hat[un], k_hat[un]], axis=0) for un in units}
    tci2 = lax.broadcasted_iota(jnp.int32, (C, 2 * C), 0)
    sci2 = lax.broadcasted_iota(jnp.int32, (C, 2 * C), 1) % C
    strict2 = sci2 < tci2
    incl2 = sci2 <= tci2
    n_pow, a_rb, top, bot = {}, {}, {}, {}
    for (s, p, hh) in uheads:
        lm = jnp.where(in_h0 if hh == 0 else ~in_h0, left[s, p], 0.0)
        quad = _mm_nt(lm, right[s, p])
        top[s, p, hh] = jnp.where(strict2, quad[:C], 0.0)
        bot[s, p, hh] = jnp.where(incl2, quad[C:], 0.0)
        n_pow[s, p, hh] = top[s, p, hh][:, :C]
        a_rb[s, p, hh] = bot[s, p, hh][:, :C]

    def by_head(mats, un, x, mask):
        return jnp.where(mask, _mm(mats[un + (0,)], x), _mm(mats[un + (1,)], x))

    zv = {un: jnp.concatenate([jnp.zeros((C, PW), F32), v_u[un]], axis=0) for un in units}
    xs = {un: jnp.concatenate([a_t[un], by_head(top, un, zv[un], in_h0)], axis=1)
          for un in units}
    y_v = {un: by_head(bot, un, zv[un], in_h0) for un in units}
    n_levels = C.bit_length() - 1
    for lvl in range(n_levels):
        xs = {un: xs[un] + by_head(n_pow, un, xs[un], in_h0_2) for un in units}
        if lvl + 1 < n_levels:
            n_pow = {hd: _mm(n_pow[hd], n_pow[hd]) for hd in uheads}

    chained = n_sub > 1
    bk_til = {un: jnp.concatenate([b_til[un], k_til[un]], axis=0) for un in units}
    if chained:
        tr_m = {un: jnp.where(blockdiag, _mm_tn(xs[un][:, :PW], b_til[un]), 0.0) for un in units}
        tr_d = {un: jnp.where(blockdiag,
                              _mm_tn(jnp.concatenate([xs[un][:, PW:], v_u[un]], axis=0),
                                     bk_til[un]), 0.0) for un in units}

    st = [state_ref[p] for p in pairs]
    y_u = {}
    pending = None
    for s in list(subs) + [None]:
        if s is not None:
            if chained:
                chain = [_mm(st[p], tr_m[s, p]) + tr_d[s, p] for p in pairs]
            u_p = [_mm_nt(xs[s, p][:, :PW], st[p]) + xs[s, p][:, PW:] for p in pairs]
            rs_p = [_mm_nt(r_t[s, p], st[p]) for p in pairs]
            if not chained:
                chain = [jnp.where(blockdiag,
                                   _mm_tn(jnp.concatenate([u_p[p], v_u[s, p]], axis=0),
                                          bk_til[s, p]), 0.0) for p in pairs]
        if pending is not None:
            s0, u0, rs0 = pending
            for p in pairs:
                y_u[s0, p] = rs0[p] + by_head(a_rb, (s0, p), u0[p], in_h0) + y_v[s0, p]
        if s is not None:
            st = [st[p] * pc_u[s, p] + chain[p] for p in pairs]
            pending = (s, u_p, rs_p)
    for p in pairs:
        state_ref[p] = st[p]

    inv_n = 1.0 / HEAD_DIM
    mu_y = head_sums(y_u)
    yc = {un: y_u[un] - mu_y[un] * inv_n for un in units}
    var = head_sums({un: yc[un] * yc[un] for un in units})
    rk_row = rk_ref[...]
    bon = head_sums({(s, p): r_u[s, p] * k_u[s, p] * rk_row[:, p * PW:(p + 1) * PW]
                     for (s, p) in units})
    for (s, p) in units:
        cols = slice(p * PW, (p + 1) * PW)
        rows = slice(s * C, (s + 1) * C)
        yn = yc[s, p] * lax.rsqrt(var[s, p] * inv_n + GN_EPS) * lg_ref[:, cols] + lb_ref[:, cols]
        o_ref[rows, cols] = ((yn + bon[s, p] * v_u[s, p]) * gate[rows, cols]).astype(o_ref.dtype)

    @pl.when(c == n_steps - 1)
    def _():
        for h in range(N_HEADS):
            lo = (h % 2) * HEAD_DIM
            s_out_ref[0, h] = state_ref[h // 2, lo:lo + HEAD_DIM, lo:lo + HEAD_DIM]
        shift_out_ref[0] = pr[R - 1:R, :]


def _rwkv(p_all, shift_in, s0, lw, *, row0, n_seq, seq_len, chunk, n_sub):
    rows = chunk * n_sub
    n_steps = seq_len // rows
    rb0 = row0 // rows
    vec = lambda n: pl.BlockSpec((1, n), lambda b, c: (0, 0))
    mat = lambda k, n: pl.BlockSpec((k, n), lambda b, c: (0, 0))
    kern = functools.partial(_rwkv_kernel, chunk=chunk, n_sub=n_sub, n_steps=n_steps)
    state_spec = pl.BlockSpec((1, N_HEADS, HEAD_DIM, HEAD_DIM), lambda b, c: (b, 0, 0, 0))
    shift_spec = pl.BlockSpec((1, 1, R_COLS), lambda b, c: (b, 0, 0))
    return pl.pallas_call(
        kern,
        grid=(n_seq, n_steps),
        in_specs=[
            pl.BlockSpec((rows, R_COLS), lambda b, c: (rb0 + b * n_steps + c, C32_R // R_COLS)),
            shift_spec, state_spec,
            vec(R_COLS), vec(W_H), mat(DECAY_LORA, W_H), vec(W_H), mat(AAA_LORA, W_H),
            mat(GATE_LORA, W_H), vec(W_H), vec(W_H), vec(W_H), vec(W_H), vec(W_H)],
        out_specs=[pl.BlockSpec((rows, W_H), lambda b, c: (b * n_steps + c, 0)),
                   state_spec, shift_spec],
        out_shape=[jax.ShapeDtypeStruct((n_seq * seq_len, W_H), BF16),
                   jax.ShapeDtypeStruct((n_seq, N_HEADS, HEAD_DIM, HEAD_DIM), F32),
                   jax.ShapeDtypeStruct((n_seq, 1, R_COLS), F32)],
        scratch_shapes=[pltpu.VMEM((N_PAIRS, LANES, LANES), F32),
                        pltpu.VMEM((1, R_COLS), F32)],
        compiler_params=_cparams(("parallel", "arbitrary")),
        name=f"rwkv_c{chunk}",
    )(p_all, shift_in, s0, lw['mu'], lw['w0'], lw['w2'], lw['a0'], lw['a2'], lw['g2'],
      lw['k_k'], lw['k_a'], lw['r_k'], lw['lnx_g'], lw['lnx_b'])


def _place_matrices():
    pk = np.zeros((W_H, N_HEADS * LANES), np.float32)
    pc = np.zeros((3 * LANES, N_HEADS * LANES), np.float32)
    ones = np.zeros((1, N_HEADS * LANES), np.float32)
    for h in range(N_HEADS):
        for j in range(HEAD_DIM):
            pk[h * HEAD_DIM + j, h * LANES + j] = 1.0
        for piece in range(3):
            pc[piece * LANES + h, h * LANES + CK_LANE0 + piece] = 1.0
            ones[0, h * LANES + CQ_LANE0 + piece] = 1.0
    return jnp.asarray(pk, BF16), jnp.asarray(pc, BF16), jnp.asarray(ones, F32)


def _fox_prep_kernel(k_ref, v_ref, fl_ref, bf_ref, pk_ref, pc_ref, ones_ref, kprev_ref, vprev_ref,
                     kall_ref, vall_ref, logft_ref, c_ref, kaug_ref, vt_ref, carry_ref):
    del kprev_ref, vprev_ref

    @pl.when(pl.program_id(1) == 0)
    def _():
        carry_ref[...] = jnp.zeros_like(carry_ref)

    tc = fl_ref.shape[0]
    k = k_ref[...]
    v_t = v_ref[...].T
    kall_ref[0, 0] = k.T
    vall_ref[0, 0] = v_t
    ones_half = jnp.ones((HEAD_DIM, tc), F32)
    for h in range(N_HEADS):
        vh = v_t[h * HEAD_DIM:(h + 1) * HEAD_DIM]
        grp = [vh, ones_half] if h % 2 == 0 else [ones_half, vh]
        vt_ref[0, h * LANES:(h + 1) * LANES, :] = jnp.concatenate(grp, axis=0).astype(BF16)
    logf = _log_sigmoid(fl_ref[...] + bf_ref[...])
    logft_ref[0] = logf.T[0:N_HEADS, :]
    ti = lax.broadcasted_iota(jnp.int32, (tc, tc), 0)
    si = lax.broadcasted_iota(jnp.int32, (tc, tc), 1)
    c = _mm_exact_rhs((si <= ti).astype(BF16), logf) + carry_ref[...]
    carry_ref[...] = c[tc - 1:tc, :]
    c2 = c * LOG2E
    c_ref[...] = c2
    pieces = jnp.concatenate(_split3(-c2), axis=1)
    kaug = (jnp.dot(k.astype(BF16), pk_ref[...], preferred_element_type=F32)
            + jnp.dot(pieces, pc_ref[...], preferred_element_type=F32) + ones_ref[...])
    kaug_ref[...] = kaug.astype(BF16)


def _fox_prep(p_all, bf_pad, places, k_prev, v_prev, layer, *, n_seq, seq_len):
    tc = _tile(seq_len, (PREP_BLOCK, 256, 128))
    nb = seq_len // tc
    rows = n_seq * seq_len
    const = lambda a: pl.BlockSpec(a.shape, lambda b, j: (0, 0))
    rowblk = lambda n: pl.BlockSpec((tc, n), lambda b, j: (b * nb + j, 0))
    stacked = pl.BlockSpec((1, 1, W_H, tc), lambda b, j: (layer, b, 0, j))
    return pl.pallas_call(
        _fox_prep_kernel,
        grid=(n_seq, nb),
        in_specs=[pl.BlockSpec((tc, W_H), lambda b, j: (b * nb + j, C32_K // W_H)),
                  pl.BlockSpec((tc, W_H), lambda b, j: (b * nb + j, C32_V // W_H)),
                  pl.BlockSpec((tc, LANES), lambda b, j: (b * nb + j, C32_FL // LANES)),
                  pl.BlockSpec((1, LANES), lambda b, j: (0, 0)),
                  const(places[0]), const(places[1]), const(places[2]),
                  pl.BlockSpec(memory_space=pl.ANY), pl.BlockSpec(memory_space=pl.ANY)],
        out_specs=[stacked, stacked,
                   pl.BlockSpec((1, N_HEADS, tc), lambda b, j: (b, 0, j)),
                   rowblk(LANES), rowblk(N_HEADS * LANES),
                   pl.BlockSpec((1, N_HEADS * LANES, tc), lambda b, j: (b, 0, j))],
        out_shape=[jax.ShapeDtypeStruct(k_prev.shape, F32),
                   jax.ShapeDtypeStruct(v_prev.shape, F32),
                   jax.ShapeDtypeStruct((n_seq, N_HEADS, seq_len), F32),
                   jax.ShapeDtypeStruct((rows, LANES), F32),
                   jax.ShapeDtypeStruct((rows, N_HEADS * LANES), BF16),
                   jax.ShapeDtypeStruct((n_seq, N_HEADS * LANES, seq_len), BF16)],
        scratch_shapes=[pltpu.VMEM((1, LANES), F32)],
        input_output_aliases={7: 0, 8: 1},
        compiler_params=_cparams(("parallel", "arbitrary")),
        name="fox_prep",
    )(p_all, p_all, p_all, bf_pad, *places, k_prev, v_prev)


def _fox_prompt_kernel(q_ref, c_ref, kaug_ref, vt_ref, o_ref, m_ref, acc_ref, sa_ref, sb_ref,
                       *, blk):
    hp = pl.program_id(1)
    i = pl.program_id(2)
    lane = lax.broadcasted_iota(jnp.int32, (1, LANES), 1)
    c_blk = c_ref[...]
    q_all = q_ref[...].astype(F32) * (SCALE * LOG2E)
    q_aug = []
    for hh in range(2):
        q_h = q_all if hh == 0 else pltpu.roll(q_all, HEAD_DIM, 1)
        cq = jnp.sum(jnp.where(lane == 2 * hp + hh, c_blk, 0.0), axis=-1, keepdims=True)
        hi, mid, lo = _split3(cq)
        aug = jnp.where(lane < HEAD_DIM, q_h,
                        jnp.where(lane == CQ_LANE0, hi.astype(F32),
                                  jnp.where(lane == CQ_LANE0 + 1, mid.astype(F32),
                                            jnp.where(lane == CQ_LANE0 + 2, lo.astype(F32),
                                                      jnp.where(lane < CK_LANE0 + 3, 1.0, 0.0)))))
        q_aug.append(aug.astype(BF16))

    m_ref[...] = jnp.full(m_ref.shape, -jnp.inf, F32)
    acc_ref[...] = jnp.zeros(acc_ref.shape, F32)
    key_i = lax.broadcasted_iota(jnp.int32, (blk, blk), 0)
    qry_i = lax.broadcasted_iota(jnp.int32, (blk, blk), 1)
    causal = key_i <= qry_i

    def scores(j, dst_ref):
        start = pl.multiple_of(j * blk, blk)
        for hh in range(2):
            dst_ref[hh] = lax.dot_general(
                kaug_ref[pl.ds(start, blk), hh * LANES:(hh + 1) * LANES], q_aug[hh],
                (((1,), (1,)), ((), ())), preferred_element_type=F32)

    def softmax_update(j, src_ref, masked):
        start = pl.multiple_of(j * blk, blk)
        for hh in range(2):
            vt = vt_ref[0, hh * LANES:(hh + 1) * LANES, pl.ds(start, blk)]
            st = src_ref[hh]
            if masked:
                st = jnp.where(causal, st, -jnp.inf)
            m_old = m_ref[hh]
            m_new = jnp.maximum(m_old, jnp.max(st, axis=0, keepdims=True))
            alpha = jnp.exp2(m_old - m_new)
            pt = jnp.exp2(st - m_new).astype(BF16)
            m_ref[hh] = m_new
            acc_ref[hh] = alpha * acc_ref[hh] + jnp.dot(vt, pt, preferred_element_type=F32)

    scores(0, sa_ref)

    def body(t, carry):
        j = 2 * t
        scores(j + 1, sb_ref)
        softmax_update(j, sa_ref, False)
        scores(j + 2, sa_ref)
        softmax_update(j + 1, sb_ref, False)
        return carry

    lax.fori_loop(0, i // 2, body, 0)

    @pl.when(i % 2 == 0)
    def _():
        softmax_update(i, sa_ref, True)

    @pl.when(i % 2 == 1)
    def _():
        scores(i, sb_ref)
        softmax_update(i - 1, sa_ref, False)
        softmax_update(i, sb_ref, True)

    chan = lax.broadcasted_iota(jnp.int32, (LANES, 1), 0)
    acc0 = acc_ref[0]
    acc1 = acc_ref[1]
    out_t = jnp.where(chan < HEAD_DIM, acc0 / acc0[HEAD_DIM:HEAD_DIM + 1, :], acc1 / acc1[0:1, :])
    o_ref[...] = out_t.T.astype(o_ref.dtype)


def _fox_prompt(p16, c, kaug, v_t, *, n_seq, seq_len):
    blk = _tile(seq_len, (ATT_BLOCK, 256, 128))
    nq = seq_len // blk
    kern = functools.partial(_fox_prompt_kernel, blk=blk)
    return pl.pallas_call(
        kern,
        grid=(n_seq, N_PAIRS, nq),
        in_specs=[pl.BlockSpec((blk, LANES), lambda b, hp, i: (b * nq + i, C16_Q // LANES + hp)),
                  pl.BlockSpec((blk, LANES), lambda b, hp, i: (b * nq + i, 0)),
                  pl.BlockSpec((seq_len, KAUG_W), lambda b, hp, i: (b, hp)),
                  pl.BlockSpec((1, 2 * LANES, seq_len), lambda b, hp, i: (b, hp, 0))],
        out_specs=pl.BlockSpec((blk, LANES), lambda b, hp, i: (b * nq + i, hp)),
        out_shape=jax.ShapeDtypeStruct((n_seq * seq_len, W_H), BF16),
        scratch_shapes=[pltpu.VMEM((2, 1, blk), F32), pltpu.VMEM((2, LANES, blk), F32),
                        pltpu.VMEM((2, blk, blk), F32), pltpu.VMEM((2, blk, blk), F32)],
        compiler_params=_cparams(("parallel", "parallel", "arbitrary")),
        name="fox_prompt",
    )(p16, c, kaug, v_t)


def _fox_sample_kernel(q_ref, k_ref, v_ref, fl_ref, bf_ref, ckt_ref, cvt_ref, clt_ref,
                       kprev_ref, vprev_ref, o_ref, logf_ref, kall_ref, vall_ref, *, t_new, past):
    del kprev_ref, vprev_ref
    T = t_new
    kall_ref[0] = k_ref[...]
    vall_ref[0] = v_ref[...]
    logf = _log_sigmoid(fl_ref[...] + bf_ref[...])
    logf_ref[...] = logf
    logf_pad = jnp.concatenate([logf, jnp.zeros((LANES - T, LANES), F32)], axis=0)
    ti = lax.broadcasted_iota(jnp.int32, (LANES, LANES), 0)
    si = lax.broadcasted_iota(jnp.int32, (LANES, LANES), 1)
    cnew_pad = _mm_exact_rhs((si <= ti).astype(BF16), logf_pad)
    cnew_t = cnew_pad.T
    cnew = cnew_pad[0:T, :]
    suf = clt_ref[0, 0]
    pos = lax.broadcasted_iota(jnp.int32, suf.shape, 1)
    total = suf
    d = 1
    while d < past:
        total = total + jnp.where(pos + d < past, pltpu.roll(total, past - d, 1), 0.0)
        d *= 2
    suf = total - suf

    lane = lax.broadcasted_iota(jnp.int32, (T, LANES), 1)
    rowi = lax.broadcasted_iota(jnp.int32, (T, T), 0)
    coli = lax.broadcasted_iota(jnp.int32, (T, T), 1)
    for h in range(N_HEADS):
        sl = slice(h * HEAD_DIM, (h + 1) * HEAD_DIM)
        q = q_ref[:, sl].astype(F32) * SCALE
        cq = jnp.sum(jnp.where(lane == h, cnew, 0.0), axis=-1, keepdims=True)
        s_c = _mm(q, ckt_ref[0, 0, h]) + (cq + suf[h:h + 1, :])
        s_n = _mm_nt(q, k_ref[:, sl]) + (cq - cnew_t[h:h + 1, 0:T])
        s_n = jnp.where(coli <= rowi, s_n, -jnp.inf)
        m = jnp.maximum(jnp.max(s_c, axis=-1, keepdims=True), jnp.max(s_n, axis=-1, keepdims=True))
        p_c = jnp.exp(s_c - m)
        p_n = jnp.exp(s_n - m)
        l = jnp.sum(p_c, axis=-1, keepdims=True) + jnp.sum(p_n, axis=-1, keepdims=True)
        o = _mm_nt(p_c, cvt_ref[0, 0, h]) + _mm(p_n, v_ref[:, sl])
        o_ref[:, sl] = (o / l).astype(o_ref.dtype)


def _fox_sample(p16, p32, bf_pad, cache_kt, cache_vt, cache_logf_t, k_prev, v_prev, layer,
                *, row0, n_seq, t_new):
    past = cache_kt.shape[4]
    rb0 = row0 // t_new
    blk512 = lambda col: pl.BlockSpec((t_new, W_H), lambda b: (rb0 + b, col // W_H))
    cache_spec = pl.BlockSpec((1, 1, N_HEADS, HEAD_DIM, past), lambda b: (layer, b, 0, 0, 0))
    stacked = pl.BlockSpec((1, t_new, W_H), lambda b: (layer, b, 0))
    return pl.pallas_call(
        functools.partial(_fox_sample_kernel, t_new=t_new, past=past),
        grid=(n_seq,),
        in_specs=[blk512(C16_Q), blk512(C32_K), blk512(C32_V),
                  pl.BlockSpec((t_new, LANES), lambda b: (rb0 + b, C32_FL // LANES)),
                  pl.BlockSpec((1, LANES), lambda b: (0, 0)),
                  cache_spec, cache_spec,
                  pl.BlockSpec((1, 1, N_HEADS, past), lambda b: (layer, b, 0, 0)),
                  pl.BlockSpec(memory_space=pl.ANY), pl.BlockSpec(memory_space=pl.ANY)],
        out_specs=[pl.BlockSpec((t_new, W_H), lambda b: (b, 0)),
                   pl.BlockSpec((t_new, LANES), lambda b: (b, 0)), stacked, stacked],
        out_shape=[jax.ShapeDtypeStruct((n_seq * t_new, W_H), BF16),
                   jax.ShapeDtypeStruct((n_seq * t_new, LANES), F32),
                   jax.ShapeDtypeStruct(k_prev.shape, F32),
                   jax.ShapeDtypeStruct(v_prev.shape, F32)],
        input_output_aliases={8: 2, 9: 3},
        compiler_params=_cparams(("parallel",)),
        name="fox_sample",
    )(p16, p32, p32, p32, bf_pad, cache_kt, cache_vt, cache_logf_t, k_prev, v_prev)


def _merge_kernel(x_ref, oap_ref, oas_ref, obp_ref, obs_ref, ga_ref, gb_ref, pa_ref, pb_ref, wo_ref,
                  o_ref, *, n_prompt_tiles):
    is_sample = pl.program_id(0) >= n_prompt_tiles
    oa = jnp.where(is_sample, oas_ref[...], oap_ref[...])
    ob = jnp.where(is_sample, obs_ref[...], obp_ref[...])
    ma = jnp.dot(oa, pa_ref[...], preferred_element_type=F32)
    mb = jnp.dot(ob, pb_ref[...], preferred_element_type=F32)
    mix = _sigmoid(ga_ref[...].astype(F32)) * ma + _sigmoid(gb_ref[...].astype(F32)) * mb
    o_ref[...] = x_ref[...] + jnp.dot(mix.astype(BF16), wo_ref[...], preferred_element_type=F32)


def _merge(x, oa_p, oa_s, ob_p, ob_s, p16, pa, pb, wo, tm):
    m, d = x.shape
    npt = oa_p.shape[0] // tm
    row = lambda n, col=0: pl.BlockSpec((tm, n), lambda i: (i, col))
    prompt = pl.BlockSpec((tm, W_H), lambda i: (jnp.minimum(i, npt - 1), 0))
    sample = pl.BlockSpec((tm, W_H), lambda i: (jnp.maximum(i - npt, 0), 0))
    full = lambda a: pl.BlockSpec(a.shape, lambda i: (0, 0))
    return pl.pallas_call(
        functools.partial(_merge_kernel, n_prompt_tiles=npt),
        grid=(m // tm,),
        in_specs=[row(d), prompt, sample, prompt, sample, row(d, C16_GA // d), row(d, C16_GB // d),
                  full(pa), full(pb), full(wo)],
        out_specs=row(d),
        out_shape=jax.ShapeDtypeStruct((m, d), F32),
        compiler_params=_cparams(("parallel",)),
        name="merge",
    )(x, oa_p, oa_s, ob_p, ob_s, p16, p16, pa, pb, wo)


def _ffn_kernel(x_ref, g_ref, wg_ref, wu_ref, wd_ref, fg_ref, *rest, final, n_prompt_tiles):
    if final:
        o_ref, os_ref, h_ref, acc_ref = rest
    else:
        o_ref, h_ref, acc_ref = rest
    f = pl.program_id(1)

    @pl.when(f == 0)
    def _():
        x = x_ref[...]
        y = x * lax.rsqrt(jnp.mean(x * x, axis=-1, keepdims=True) + EPS) * g_ref[...]
        h_ref[...] = y.astype(BF16)
        acc_ref[...] = x

    h = h_ref[...]
    gate = jnp.dot(h, wg_ref[...], preferred_element_type=F32)
    up = jnp.dot(h, wu_ref[...], preferred_element_type=F32)
    act = gate * _sigmoid(gate) * up
    acc_ref[...] += jnp.dot(act.astype(BF16), wd_ref[...], preferred_element_type=F32)

    is_last = f == pl.num_programs(1) - 1
    if not final:
        @pl.when(is_last)
        def _():
            o_ref[...] = acc_ref[...]
    else:
        def normed():
            y = acc_ref[...]
            return y * lax.rsqrt(jnp.mean(y * y, axis=-1, keepdims=True) + EPS) * fg_ref[...]

        @pl.when(is_last & (pl.program_id(0) < n_prompt_tiles))
        def _():
            o_ref[...] = normed()

        @pl.when(is_last & (pl.program_id(0) >= n_prompt_tiles))
        def _():
            os_ref[...] = normed()


def _ffn(x, g, wg, wu, wd, fg, tm, tf, final, m_prompt):
    m, d = x.shape
    dff = wg.shape[1]
    npt = m_prompt // tm
    if final:
        out_specs = [pl.BlockSpec((tm, d), lambda i, f: (jnp.minimum(i, npt - 1), 0)),
                     pl.BlockSpec((tm, d), lambda i, f: (jnp.maximum(i - npt, 0), 0))]
        out_shape = [jax.ShapeDtypeStruct((m_prompt, d), F32),
                     jax.ShapeDtypeStruct((m - m_prompt, d), F32)]
    else:
        out_specs = pl.BlockSpec((tm, d), lambda i, f: (i, 0))
        out_shape = jax.ShapeDtypeStruct((m, d), F32)
    return pl.pallas_call(
        functools.partial(_ffn_kernel, final=final, n_prompt_tiles=npt),
        grid=(m // tm, dff // tf),
        in_specs=[pl.BlockSpec((tm, d), lambda i, f: (i, 0)),
                  pl.BlockSpec((1, d), lambda i, f: (0, 0)),
                  pl.BlockSpec((d, tf), lambda i, f: (0, f)),
                  pl.BlockSpec((d, tf), lambda i, f: (0, f)),
                  pl.BlockSpec((tf, d), lambda i, f: (f, 0)),
                  pl.BlockSpec((1, d), lambda i, f: (0, 0))],
        out_specs=out_specs,
        out_shape=out_shape,
        scratch_shapes=[pltpu.VMEM((tm, d), BF16), pltpu.VMEM((tm, d), F32)],
        compiler_params=_cparams(("arbitrary" if final else "parallel", "arbitrary")),
        name="ffn",
    )(x, g, wg, wu, wd, fg)


def kernel(x_prompt, x_sample, cache_fox_k, cache_fox_v, cache_fox_logf, state_rwkv, state_shift,
           norm1_g, w_in, rwkv_mu, rwkv_w0, rwkv_w2, rwkv_a0, rwkv_a2, rwkv_g2, rwkv_k_k, rwkv_k_a,
           rwkv_r_k, rwkv_lnx_g, rwkv_lnx_b, fox_bf, p_a, p_b, w_out, norm2_g, w_gate, w_up, w_down,
           final_g):
    n_b, seq, d = x_prompt.shape
    n_db, t_new, _ = x_sample.shape
    depth = w_in.shape[0]
    d_ff = w_gate.shape[2]
    m_prompt = n_b * seq
    m_sample = n_db * t_new
    m_all = m_prompt + m_sample

    o_q = R_COLS
    o_fl = R_COLS + 3 * W_H
    o_g = o_fl + N_HEADS
    w_t = jnp.swapaxes(w_in, 1, 2)
    w_cat_t = jnp.concatenate(
        [w_t[:, o_g:o_g + 2 * d], w_t[:, o_q:o_q + 3 * W_H], w_t[:, 0:R_COLS],
         w_t[:, o_fl:o_fl + N_HEADS],
         jnp.zeros((depth, P_COLS - C32_FL - N_HEADS, d), w_in.dtype)], axis=1).astype(BF16)
    bf_pad = jnp.pad(fox_bf, ((0, 0), (0, LANES - N_HEADS)))
    pa_b, pb_b, wo_b = p_a.astype(BF16), p_b.astype(BF16), w_out.astype(BF16)
    wg_b, wu_b, wd_b = w_gate.astype(BF16), w_up.astype(BF16), w_down.astype(BF16)
    w2_b, a2_b, g2_b = rwkv_w2.astype(BF16), rwkv_a2.astype(BF16), rwkv_g2.astype(BF16)
    cache_kt = jnp.transpose(cache_fox_k, (0, 1, 3, 4, 2))
    cache_vt = jnp.transpose(cache_fox_v, (0, 1, 3, 4, 2))
    cache_logf_t = jnp.swapaxes(cache_fox_logf, 2, 3)
    places = _place_matrices()

    tm = _tile(math.gcd(m_prompt, m_sample))
    tm_w = _tile(m_all, (1536, 512, 256, 128, 64, 32, 16, 8))
    tn = _tile(P_COLS, (1408, 512))
    tm_f = _tile(m_all, (768, 512, 256, 128, 64, 32, 16, 8))
    tf = _tile(d_ff, (1408, 256, 128))
    x = jnp.concatenate([x_prompt.reshape(m_prompt, d), x_sample.reshape(m_sample, d)], axis=0)
    zero_shift = jnp.zeros((n_b, 1, R_COLS), F32)
    zero_state = jnp.zeros((n_b, N_HEADS, HEAD_DIM, HEAD_DIM), F32)
    n_sub = PROMPT_SUBCHUNKS if seq % (PROMPT_CHUNK * PROMPT_SUBCHUNKS) == 0 else 1

    outs = {k: [] for k in ('lfp', 'sp', 'shp', 'lfd', 'sd', 'shd')}
    kp = jnp.zeros((depth, n_b, W_H, seq), F32)
    vp = jnp.zeros((depth, n_b, W_H, seq), F32)
    kd = jnp.zeros((depth, m_sample, W_H), F32)
    vd = jnp.zeros((depth, m_sample, W_H), F32)
    for l in range(depth):
        row = lambda a: a[l].reshape(1, -1)
        lw = dict(mu=row(rwkv_mu), w0=row(rwkv_w0), w2=w2_b[l], a0=row(rwkv_a0), a2=a2_b[l],
                  g2=g2_b[l], k_k=row(rwkv_k_k), k_a=row(rwkv_k_a), r_k=row(rwkv_r_k),
                  lnx_g=row(rwkv_lnx_g), lnx_b=row(rwkv_lnx_b))
        p16 = p32 = _norm_matmul(x, row(norm1_g), w_cat_t[l], tm_w, tn)

        oa_p, s_p, sh_p = _rwkv(p32, zero_shift, zero_state, lw, row0=0, n_seq=n_b,
                                seq_len=seq, chunk=PROMPT_CHUNK, n_sub=n_sub)
        oa_s, s_d, sh_d = _rwkv(p32, state_shift[l], state_rwkv[l], lw, row0=m_prompt,
                                n_seq=n_db, seq_len=t_new, chunk=t_new, n_sub=1)

        kp, vp, logft_p, c_p, kaug, v_t = _fox_prep(p32, bf_pad[l:l + 1], places, kp, vp, l,
                                                    n_seq=n_b, seq_len=seq)
        ob_p = _fox_prompt(p16, c_p, kaug, v_t, n_seq=n_b, seq_len=seq)
        ob_s, logf_d, kd, vd = _fox_sample(p16, p32, bf_pad[l:l + 1], cache_kt, cache_vt,
                                           cache_logf_t, kd, vd, l, row0=m_prompt, n_seq=n_db,
                                           t_new=t_new)

        x = _merge(x, oa_p, oa_s, ob_p, ob_s, p16, pa_b[l], pb_b[l], wo_b[l], tm)
        last = l == depth - 1
        x = _ffn(x, row(norm2_g), wg_b[l], wu_b[l], wd_b[l], final_g.reshape(1, -1),
                 tm if last else tm_f, tf, last, m_prompt)

        outs['lfp'].append(logft_p)
        outs['lfd'].append(logf_d[:, :N_HEADS].reshape(n_db, t_new, N_HEADS))
        outs['sp'].append(s_p)
        outs['shp'].append(sh_p)
        outs['sd'].append(s_d)
        outs['shd'].append(sh_d)

    st = {k: jnp.stack(v) for k, v in outs.items()}
    prompt_heads = lambda a: jnp.transpose(a.reshape(depth, n_b, N_HEADS, HEAD_DIM, seq),
                                           (0, 1, 4, 2, 3))
    sample_heads = lambda a: a.reshape(depth, n_db, t_new, N_HEADS, HEAD_DIM)
    y_prompt = x[0].reshape(n_b, seq, d)
    y_sample = x[1].reshape(n_db, t_new, d)
    return (y_prompt, y_sample, prompt_heads(kp), prompt_heads(vp), jnp.swapaxes(st['lfp'], 2, 3),
            st['sp'], st['shp'], sample_heads(kd), sample_heads(vd), st['lfd'], st['sd'],
            st['shd'])
```
